```python
import jax, jax.numpy as jnp
from jax import lax
import numpy as np

D_MODEL = 4096
BATCH = 1
SEQ = 16384
DEPTH = 2

N_A_LAYERS = DEPTH // 2
N_B_LAYERS = DEPTH - N_A_LAYERS

D_FF = 11008
RMS_EPS = 1e-6
ROPE_THETA = 10000.0
NEG = -1e30

A_HEADS = 32
A_KV_HEADS = 8
A_HEAD_DIM = D_MODEL // A_HEADS
A_Q_WIDTH = A_HEADS * A_HEAD_DIM
A_KV_WIDTH = A_KV_HEADS * A_HEAD_DIM
MOBA_BLOCK = 256
MOBA_TOPK = 3
MOBA_QCHUNK = 32

B_HEADS = 64
B_KV_HEADS = 8
B_HEAD_DIM = D_MODEL // B_HEADS
B_Q_WIDTH = B_HEADS * B_HEAD_DIM
B_KV_WIDTH = B_KV_HEADS * B_HEAD_DIM
SWA_WINDOW = 128
SWA_BLOCK = 128

kernel_name = "yoco_moba_swa_sink_macaron"


def rms_norm(x, g):
    x32 = x.astype(jnp.float32)
    y = x32 * lax.rsqrt(jnp.mean(x32 * x32, axis=-1, keepdims=True) + RMS_EPS)
    return (y * g.astype(jnp.float32)).astype(x.dtype)


def rope_tables(seq, dim):
    inv = 1.0 / (ROPE_THETA ** (jnp.arange(0, dim, 2, dtype=jnp.float32) / dim))
    ang = jnp.arange(seq, dtype=jnp.float32)[:, None] * inv[None, :]
    return jnp.cos(ang), jnp.sin(ang)


def apply_rope(x, cos, sin):
    x32 = x.astype(jnp.float32)
    x1, x2 = jnp.split(x32, 2, axis=-1)
    c = cos[None, :, None, :]
    s = sin[None, :, None, :]
    return jnp.concatenate([x1 * c - x2 * s, x2 * c + x1 * s], axis=-1).astype(x.dtype)


def swiglu(h, w_in, w_out):
    g, u = jnp.split(h @ w_in, 2, axis=-1)
    return (jax.nn.silu(g) * u) @ w_out


def moba_attention(q, k, v):
    b, s, h, dh = q.shape
    hkv = k.shape[2]
    g = h // hkv
    L = MOBA_BLOCK
    nb = -(-s // L)
    pad = nb * L - s
    topk = min(MOBA_TOPK, nb)
    scale = dh ** -0.5
    padw = ((0, 0), (0, pad), (0, 0), (0, 0))
    kb = jnp.pad(k, padw).reshape(b, nb, L, hkv, dh).transpose(0, 3, 1, 2, 4)
    vb = jnp.pad(v, padw).reshape(b, nb, L, hkv, dh).transpose(0, 3, 1, 2, 4)
    kmean = jnp.mean(kb.astype(jnp.float32), axis=3)

    nq = s // MOBA_QCHUNK
    qc = q.reshape(b, nq, MOBA_QCHUNK, hkv, g, dh).transpose(1, 0, 2, 3, 4, 5)
    chunk_ids = jnp.arange(nq, dtype=jnp.int32)
    bi = jnp.arange(b)[:, None, None, None, None]
    hi = jnp.arange(hkv)[None, None, :, None, None]
    blk = jnp.arange(nb)

    def body(args):
        qch, ci = args
        t0 = ci * MOBA_QCHUNK
        cur = t0 // L
        qpos = t0 + jnp.arange(MOBA_QCHUNK)
        q32 = qch.astype(jnp.float32)
        gate = jnp.einsum('bqhgd,bhnd->bqhgn', q32, kmean)
        gate = jnp.where(blk < cur, gate, NEG)
        _, sel = lax.top_k(gate, topk)
        valid = sel < cur
        kg = kb[bi, hi, sel].astype(jnp.float32)
        vg = vb[bi, hi, sel].astype(jnp.float32)
        s_past = jnp.einsum('bqhgd,bqhgkld->bqhgkl', q32, kg) * scale
        s_past = jnp.where(valid[..., None], s_past, NEG).reshape(b, MOBA_QCHUNK, hkv, g, topk * L)
        k_own = lax.dynamic_slice_in_dim(kb, cur, 1, axis=2)[:, :, 0].astype(jnp.float32)
        v_own = lax.dynamic_slice_in_dim(vb, cur, 1, axis=2)[:, :, 0].astype(jnp.float32)
        s_own = jnp.einsum('bqhgd,bhld->bqhgl', q32, k_own) * scale
        kpos = cur * L + jnp.arange(L)
        causal = kpos[None, :] <= qpos[:, None]
        s_own = jnp.where(causal[None, :, None, None, :], s_own, NEG)
        p = jax.nn.softmax(jnp.concatenate([s_own, s_past], axis=-1), axis=-1)
        p_own = p[..., :L]
        p_past = p[..., L:].reshape(b, MOBA_QCHUNK, hkv, g, topk, L)
        o = (jnp.einsum('bqhgl,bhld->bqhgd', p_own, v_own)
             + jnp.einsum('bqhgkl,bqhgkld->bqhgd', p_past, vg))
        return o.astype(q.dtype)

    out = lax.map(body, (qc, chunk_ids))
    return out.transpose(1, 0, 2, 3, 4, 5).reshape(b, s, h * dh)


def swa_sink_attention(q, k, v, sinks):
    b, s, h, dh = q.shape
    hkv = k.shape[2]
    g = h // hkv
    L = SWA_BLOCK
    nblk = s // L
    scale = dh ** -0.5
    qb = q.reshape(b, nblk, L, hkv, g, dh).astype(jnp.float32)
    kb = k.reshape(b, nblk, L, hkv, dh)
    vb = v.reshape(b, nblk, L, hkv, dh)
    padw = ((0, 0), (1, 0), (0, 0), (0, 0), (0, 0))
    kband = jnp.concatenate([jnp.pad(kb, padw)[:, :-1], kb], axis=2).astype(jnp.float32)
    vband = jnp.concatenate([jnp.pad(vb, padw)[:, :-1], vb], axis=2).astype(jnp.float32)
    scores = jnp.einsum('bnqhgd,bnkhd->bnhgqk', qb, kband) * scale
    n = jnp.arange(nblk)[:, None, None]
    qpos = n * L + jnp.arange(L)[None, :, None]
    kpos = n * L - L + jnp.arange(2 * L)[None, None, :]
    allowed = (kpos <= qpos) & (qpos - kpos < SWA_WINDOW) & (kpos >= 0)
    scores = jnp.where(allowed[None, :, None, None], scores, NEG)
    sink = sinks.astype(jnp.float32).reshape(hkv, g)[None, None, :, :, None, None]
    m = jnp.maximum(jnp.max(scores, axis=-1, keepdims=True), sink)
    p = jnp.exp(scores - m)
    p = p / (jnp.sum(p, axis=-1, keepdims=True) + jnp.exp(sink - m))
    o = jnp.einsum('bnhgqk,bnkhd->bnqhgd', p, vband)
    return o.reshape(b, s, h * dh).astype(q.dtype)


def setup_inputs(seed: int = 0) -> dict:
    key = jax.random.key(seed)
    ks = jax.random.split(key, 14)
    f32 = jnp.float32
    nrm = lambda k, shape, fan_in: jax.random.normal(k, shape, f32) * (fan_in ** -0.5)
    gain = lambda k, shape: 1.0 + 0.01 * jax.random.normal(k, shape, f32)
    return {
        "x": jax.random.normal(ks[0], (BATCH, SEQ, D_MODEL), f32),
        "ffn_norm": gain(ks[1], (DEPTH, 2, D_MODEL)),
        "ffn_w_in": nrm(ks[2], (DEPTH, 2, D_MODEL, 2 * D_FF), D_MODEL),
        "ffn_w_out": nrm(ks[3], (DEPTH, 2, D_FF, D_MODEL), D_FF),
        "attn_norm": gain(ks[4], (DEPTH, D_MODEL)),
        "a_w_qkv": nrm(ks[5], (N_A_LAYERS, D_MODEL, A_Q_WIDTH + 2 * A_KV_WIDTH), D_MODEL),
        "a_w_o": nrm(ks[6], (N_A_LAYERS, A_Q_WIDTH, D_MODEL), A_Q_WIDTH),
        "kv_norm": gain(ks[7], (D_MODEL,)),
        "w_kv_shared": nrm(ks[8], (D_MODEL, 2 * B_KV_WIDTH), D_MODEL),
        "b_w_q": nrm(ks[9], (N_B_LAYERS, D_MODEL, B_Q_WIDTH), D_MODEL),
        "b_sinks": 0.5 * jax.random.normal(ks[10], (N_B_LAYERS, B_HEADS), f32),
        "b_w_o": nrm(ks[11], (N_B_LAYERS, B_Q_WIDTH, D_MODEL), B_Q_WIDTH),
        "final_norm": gain(ks[12], (D_MODEL,)),
    }


def reference(x, ffn_norm, ffn_w_in, ffn_w_out, attn_norm, a_w_qkv, a_w_o, kv_norm,
              w_kv_shared, b_w_q, b_sinks, b_w_o, final_norm):
    b, s, _ = x.shape
    cos_a, sin_a = rope_tables(s, A_HEAD_DIM)
    cos_b, sin_b = rope_tables(s, B_HEAD_DIM)
    k_sh = None
    v_sh = None
    for layer in range(DEPTH):
        if layer == N_A_LAYERS:
            kv = rms_norm(x, kv_norm) @ w_kv_shared
            k_sh, v_sh = jnp.split(kv, 2, axis=-1)
            k_sh = apply_rope(k_sh.reshape(b, s, B_KV_HEADS, B_HEAD_DIM), cos_b, sin_b)
            v_sh = v_sh.reshape(b, s, B_KV_HEADS, B_HEAD_DIM)
        x = x + 0.5 * swiglu(rms_norm(x, ffn_norm[layer, 0]), ffn_w_in[layer, 0], ffn_w_out[layer, 0])
        h = rms_norm(x, attn_norm[layer])
        if layer < N_A_LAYERS:
            q, k, v = jnp.split(h @ a_w_qkv[layer], [A_Q_WIDTH, A_Q_WIDTH + A_KV_WIDTH], axis=-1)
            q = apply_rope(q.reshape(b, s, A_HEADS, A_HEAD_DIM), cos_a, sin_a)
            k = apply_rope(k.reshape(b, s, A_KV_HEADS, A_HEAD_DIM), cos_a, sin_a)
            v = v.reshape(b, s, A_KV_HEADS, A_HEAD_DIM)
            x = x + moba_attention(q, k, v) @ a_w_o[layer]
        else:
            i = layer - N_A_LAYERS
            q = apply_rope((h @ b_w_q[i]).reshape(b, s, B_HEADS, B_HEAD_DIM), cos_b, sin_b)
            x = x + swa_sink_attention(q, k_sh, v_sh, b_sinks[i]) @ b_w_o[i]
        x = x + 0.5 * swiglu(rms_norm(x, ffn_norm[layer, 1]), ffn_w_in[layer, 1], ffn_w_out[layer, 1])
    return rms_norm(x, final_norm)
```

```python
import functools

import jax
import jax.numpy as jnp
import numpy as np
from jax import lax
from jax.experimental import pallas as pl
from jax.experimental.pallas import tpu as pltpu

F32 = jnp.float32
BF16 = jnp.bfloat16

RMS_EPS = 1e-6
ROPE_THETA = 10000.0
NEG = -1e30
TAKEN = -3e38

A_HEAD_DIM = 128
A_GROUP = 4
MOBA_BLOCK = 256
MOBA_TOPK = 3

B_HEAD_DIM = 64
B_GROUP = 8
SWA_BLOCK = 128

LANES = 128
UP_TILE_N = 256
V7X_VMEM_CAP = 60000 * 1024

_NT = (((1,), (1,)), ((), ()))
_TN = (((0,), (0,)), ((), ()))


def _params(semantics, block_bytes, scratch_bytes=0):
    need = 2 * block_bytes + scratch_bytes
    return pltpu.CompilerParams(
        dimension_semantics=semantics,
        vmem_limit_bytes=int(min(V7X_VMEM_CAP, max(need, 16 * 1024 * 1024))),
    )


def _rmsnorm_kernel(x_ref, g_ref, *o_refs):
    x = x_ref[...]
    y = x * lax.rsqrt(jnp.mean(x * x, axis=-1, keepdims=True) + RMS_EPS)
    for n, o_ref in enumerate(o_refs):
        o_ref[...] = (y * g_ref[n:n + 1, :]).astype(o_ref.dtype)


def _rmsnorm(x, gains, out_dtype):
    s, d = x.shape
    n = gains.shape[0]
    tm = min(512, s)
    row = pl.BlockSpec((tm, d), lambda i: (i, 0))
    outs = pl.pallas_call(
        _rmsnorm_kernel,
        grid=(s // tm,),
        in_specs=[row, pl.BlockSpec((n, d), lambda i: (0, 0))],
        out_specs=[row] * n,
        out_shape=[jax.ShapeDtypeStruct((s, d), out_dtype)] * n,
        compiler_params=_params(("parallel",), tm * d * (4 + 4 * n), tm * d * 4),
        name="rmsnorm",
    )(x, gains)
    return outs


def _swiglu_up_kernel(h_ref, wg_ref, wu_ref, o_ref):
    h = h_ref[...]
    g = jnp.dot(h, wg_ref[...], preferred_element_type=F32)
    u = jnp.dot(h, wu_ref[...], preferred_element_type=F32)
    o_ref[...] = (g * (1.0 / (1.0 + jnp.exp(-g))) * u).astype(o_ref.dtype)


def _swiglu_up(h, w_in):
    s, d = h.shape
    f = w_in.shape[1] // 2
    tm = min(1024, s)
    tn = UP_TILE_N
    nj = f // tn
    return pl.pallas_call(
        _swiglu_up_kernel,
        grid=(s // tm, nj),
        in_specs=[
            pl.BlockSpec((tm, d), lambda i, j: (i, 0)),
            pl.BlockSpec((d, tn), lambda i, j: (0, j)),
            pl.BlockSpec((d, tn), lambda i, j: (0, j + nj)),
        ],
        out_specs=pl.BlockSpec((tm, tn), lambda i, j: (i, j)),
        out_shape=jax.ShapeDtypeStruct((s, f), BF16),
        compiler_params=_params(
            ("parallel", "arbitrary"),
            tm * d * 2 + 2 * d * tn * 2 + tm * tn * 2, 4 * tm * tn * 4),
        name="swiglu_up",
    )(h, w_in, w_in)


def _matmul_residual_kernel(a_ref, w_ref, r_ref, o_ref, *, alpha):
    acc = jnp.dot(a_ref[...], w_ref[...], preferred_element_type=F32)
    o_ref[...] = r_ref[...] + alpha * acc


def _matmul_residual(a, w, res, alpha, tm, tn):
    s, k = a.shape
    n = w.shape[1]
    tm, tn = min(tm, s), min(tn, n)
    return pl.pallas_call(
        functools.partial(_matmul_residual_kernel, alpha=alpha),
        grid=(s // tm, n // tn),
        in_specs=[
            pl.BlockSpec((tm, k), lambda i, j: (i, 0)),
            pl.BlockSpec((k, tn), lambda i, j: (0, j)),
            pl.BlockSpec((tm, tn), lambda i, j: (i, j)),
        ],
        out_specs=pl.BlockSpec((tm, tn), lambda i, j: (i, j)),
        out_shape=jax.ShapeDtypeStruct((s, n), F32),
        compiler_params=_params(
            ("parallel", "arbitrary"),
            tm * k * 2 + k * tn * 2 + 2 * tm * tn * 4, 2 * tm * tn * 4),
        name="matmul_residual",
    )(a, w, res)


def _rope_tables(seq, dim):
    half = dim // 2
    inv = 1.0 / (ROPE_THETA ** (jnp.arange(0, dim, 2, dtype=F32) / dim))
    ang = jnp.arange(seq, dtype=F32)[:, None] * inv[None, :]
    cos, sin = jnp.cos(ang), jnp.sin(ang)
    zero = jnp.zeros_like(sin)
    reps = LANES // dim
    cos_t = jnp.tile(jnp.concatenate([cos, cos], axis=-1), (1, reps))
    s_lo = jnp.tile(jnp.concatenate([-sin, zero], axis=-1), (1, reps))
    s_hi = jnp.tile(jnp.concatenate([zero, sin], axis=-1), (1, reps))
    return cos_t, s_lo, s_hi


def _proj_kernel(h_ref, w_ref, *refs, half, scale, group_major):
    acc = jnp.dot(h_ref[...], w_ref[...], preferred_element_type=F32)
    if half:
        cos_ref, slo_ref, shi_ref, o_ref = refs
        cos, s_lo, s_hi = cos_ref[...], slo_ref[...], shi_ref[...]
    else:
        (o_ref,) = refs
    for g in range(acc.shape[1] // LANES):
        x = acc[:, g * LANES:(g + 1) * LANES]
        if half:
            x = (x * cos + pltpu.roll(x, LANES - half, axis=1) * s_lo
                 + pltpu.roll(x, half, axis=1) * s_hi)
        if scale != 1.0:
            x = x * scale
        if group_major:
            o_ref[g] = x.astype(o_ref.dtype)
        else:
            o_ref[:, g * LANES:(g + 1) * LANES] = x.astype(o_ref.dtype)


def _proj(h, w, col0, ncols, *, rope=None, half=0, scale=1.0, group_major=False):
    s, k = h.shape
    tm = min(1024, s)
    tn = min(512, ncols)
    j0 = col0 // tn
    in_specs = [
        pl.BlockSpec((tm, k), lambda i, j: (i, 0)),
        pl.BlockSpec((k, tn), lambda i, j: (0, j + j0)),
    ]
    args = [h, w]
    if half:
        tab = pl.BlockSpec((tm, LANES), lambda i, j: (i, 0))
        in_specs += [tab, tab, tab]
        args += list(rope)
    if group_major:
        gpt = tn // LANES
        out_spec = pl.BlockSpec((gpt, tm, LANES), lambda i, j: (j, i, 0))
        out_shape = jax.ShapeDtypeStruct((ncols // LANES, s, LANES), BF16)
    else:
        out_spec = pl.BlockSpec((tm, tn), lambda i, j: (i, j))
        out_shape = jax.ShapeDtypeStruct((s, ncols), BF16)
    return pl.pallas_call(
        functools.partial(_proj_kernel, half=half, scale=scale, group_major=group_major),
        grid=(s // tm, ncols // tn),
        in_specs=in_specs,
        out_specs=out_spec,
        out_shape=out_shape,
        compiler_params=_params(
            ("parallel", "arbitrary"),
            tm * k * 2 + k * tn * 2 + 3 * tm * LANES * 4 + tm * tn * 2, 3 * tm * tn * 4),
        name="proj",
    )(*args)


def _moba_kernel(q_ref, k_ref, v_ref, o_ref, kmean_ref, sel_ref, m_ref, l_ref, acc_ref):
    group, blk, dh = q_ref.shape
    nb = k_ref.shape[0] // blk
    rows = group * blk
    i = pl.program_id(1)

    @pl.when(i == 0)
    def _block_means():
        def body(j, carry):
            kj = k_ref[pl.ds(pl.multiple_of(j * blk, blk), blk), :].astype(F32)
            kmean_ref[pl.ds(j, 1), :] = jnp.sum(kj, axis=0, keepdims=True) * (1.0 / blk)
            return carry
        lax.fori_loop(0, nb, body, 0)

    def q_rows():
        return q_ref[...].reshape(rows, dh)

    km = kmean_ref[...]
    km_hi = km.astype(BF16)
    km_lo = (km - km_hi.astype(F32)).astype(BF16)
    gate = (lax.dot_general(km_hi, q_rows(), _NT, preferred_element_type=F32)
            + lax.dot_general(km_lo, q_rows(), _NT, preferred_element_type=F32))
    bidx = lax.broadcasted_iota(jnp.int32, (nb, rows), 0)
    past = bidx < i
    bidx = bidx.astype(F32)
    gate = jnp.where(past, gate, NEG)
    sel = jnp.zeros((nb, rows), F32)
    for _ in range(min(MOBA_TOPK, nb)):
        top = jnp.max(gate, axis=0, keepdims=True)
        first = jnp.min(jnp.where(gate == top, bidx, float(nb)), axis=0, keepdims=True)
        pick = bidx == first
        sel = jnp.where(pick & past, 1.0, sel)
        gate = jnp.where(pick, TAKEN, gate)
    sel_ref[...] = sel

    own = pl.multiple_of(i * blk, blk)
    s = lax.dot_general(k_ref[pl.ds(own, blk), :], q_rows(), _NT, preferred_element_type=F32)
    kpos = lax.broadcasted_iota(jnp.int32, (blk, rows), 0)
    qpos = lax.broadcasted_iota(jnp.int32, (blk, rows), 1) & (blk - 1)
    s = jnp.where(kpos <= qpos, s, NEG)
    m = jnp.max(s, axis=0, keepdims=True)
    p = jnp.exp(s - m)
    m_ref[...] = m
    l_ref[...] = jnp.sum(p, axis=0, keepdims=True)
    acc_ref[...] = lax.dot_general(v_ref[pl.ds(own, blk), :], p.astype(BF16), _TN,
                                   preferred_element_type=F32)

    def past_block(j, carry):
        off = pl.multiple_of(j * blk, blk)
        s = lax.dot_general(k_ref[pl.ds(off, blk), :], q_rows(), _NT, preferred_element_type=F32)
        s = jnp.where(sel_ref[pl.ds(j, 1), :] > 0.0, s, NEG)
        m_old = m_ref[...]
        m_new = jnp.maximum(m_old, jnp.max(s, axis=0, keepdims=True))
        alpha = jnp.exp(m_old - m_new)
        p = jnp.exp(s - m_new)
        m_ref[...] = m_new
        l_ref[...] = alpha * l_ref[...] + jnp.sum(p, axis=0, keepdims=True)
        acc_ref[...] = alpha * acc_ref[...] + lax.dot_general(
            v_ref[pl.ds(off, blk), :], p.astype(BF16), _TN, preferred_element_type=F32)
        return carry
    lax.fori_loop(0, i, past_block, 0)

    out_t = acc_ref[...] / l_ref[...]
    for g in range(group):
        o_ref[:, g * dh:(g + 1) * dh] = out_t[:, g * blk:(g + 1) * blk].T.astype(o_ref.dtype)


def _moba_attention(q, k, v):
    h, s, dh = q.shape
    hkv = k.shape[0]
    group = h // hkv
    blk = MOBA_BLOCK
    nb = s // blk
    rows = group * blk
    kv_spec = pl.BlockSpec((None, s, dh), lambda hh, i: (hh, 0, 0))
    return pl.pallas_call(
        _moba_kernel,
        grid=(hkv, nb),
        in_specs=[pl.BlockSpec((group, blk, dh), lambda hh, i: (hh, i, 0)), kv_spec, kv_spec],
        out_specs=pl.BlockSpec((blk, group * dh), lambda hh, i: (i, hh)),
        out_shape=jax.ShapeDtypeStruct((s, h * dh), BF16),
        scratch_shapes=[
            pltpu.VMEM((nb, dh), F32),
            pltpu.VMEM((nb, rows), F32),
            pltpu.VMEM((1, rows), F32),
            pltpu.VMEM((1, rows), F32),
            pltpu.VMEM((dh, rows), F32),
        ],
        compiler_params=_params(
            ("arbitrary", "arbitrary"),
            rows * dh * 2 + 2 * s * dh * 2 + blk * group * dh * 2,
            (nb + dh + 2) * rows * 4 + 6 * blk * rows * 4),
        name="moba_attention",
    )(q, k, v)


def _swa_kernel(q_ref, kp_ref, ko_ref, vp_ref, vo_ref, sink_ref, o_ref):
    pairs, blk, _ = q_ref.shape
    rows = pairs * blk
    hkv = pl.program_id(0)
    n = pl.program_id(1)
    odd = (hkv % 2) == 1

    lane = lax.broadcasted_iota(jnp.int32, (2 * blk, LANES), 1)
    lo = lane < B_HEAD_DIM

    def band(prev_ref, own_ref):
        x = jnp.concatenate([prev_ref[...], own_ref[...]], axis=0)
        swapped = pltpu.roll(x, B_HEAD_DIM, axis=1)
        mine_lo = jnp.where(odd, swapped, x)
        mine_hi = jnp.where(odd, x, swapped)
        return mine_lo, mine_hi

    k_lo, k_hi = band(kp_ref, ko_ref)
    v_lo, v_hi = band(vp_ref, vo_ref)
    zero = jnp.zeros_like(k_lo)
    k_both = jnp.where(lo, k_lo, k_hi)
    v_first = jnp.where(lo, v_lo, zero)
    v_second = jnp.where(lo, zero, v_hi)

    q = q_ref[...].reshape(rows, LANES)
    qlane = lax.broadcasted_iota(jnp.int32, (rows, LANES), 1)
    qzero = jnp.zeros_like(q)
    q_first = jnp.where(qlane < B_HEAD_DIM, q, qzero)
    q_second = jnp.where(qlane < B_HEAD_DIM, qzero, q)

    kk = lax.broadcasted_iota(jnp.int32, (2 * blk, rows), 0)
    tt = lax.broadcasted_iota(jnp.int32, (2 * blk, rows), 1) & (blk - 1)
    allowed = (kk > tt) & (kk <= tt + blk) & ((kk >= blk) | (n > 0))

    def half_head(q_half, v_half, sink):
        s = lax.dot_general(k_both, q_half, _NT, preferred_element_type=F32)
        s = jnp.where(allowed, s, NEG)
        m = jnp.maximum(jnp.max(s, axis=0, keepdims=True), sink)
        p = jnp.exp(s - m)
        denom = jnp.sum(p, axis=0, keepdims=True) + jnp.exp(sink - m)
        o_t = lax.dot_general(v_half, p.astype(BF16), _TN, preferred_element_type=F32)
        return o_t / denom

    out_t = (half_head(q_first, v_first, sink_ref[0:1, :])
             + half_head(q_second, v_second, sink_ref[1:2, :]))
    for pr in range(pairs):
        o_ref[:, pr * LANES:(pr + 1) * LANES] = (
            out_t[:, pr * blk:(pr + 1) * blk].T.astype(o_ref.dtype))


def _swa_attention(q, k, v, sinks):
    npair, s, _ = q.shape
    hkv = k.shape[1] // B_HEAD_DIM
    pairs = npair // hkv
    blk = SWA_BLOCK
    nblk = s // blk
    rows = pairs * blk
    sink_rows = jnp.repeat(sinks.astype(F32).reshape(hkv, pairs, 2).transpose(0, 2, 1), blk, axis=-1)
    prev = pl.BlockSpec((blk, LANES), lambda hh, n: (jnp.maximum(n - 1, 0), hh // 2))
    own = pl.BlockSpec((blk, LANES), lambda hh, n: (n, hh // 2))
    return pl.pallas_call(
        _swa_kernel,
        grid=(hkv, nblk),
        in_specs=[
            pl.BlockSpec((pairs, blk, LANES), lambda hh, n: (hh, n, 0)),
            prev, own, prev, own,
            pl.BlockSpec((None, 2, rows), lambda hh, n: (hh, 0, 0)),
        ],
        out_specs=pl.BlockSpec((blk, pairs * LANES), lambda hh, n: (n, hh)),
        out_shape=jax.ShapeDtypeStruct((s, npair * LANES), BF16),
        compiler_params=_params(
            ("parallel", "parallel"),
            2 * rows * LANES * 2 + 4 * blk * LANES * 2 + 2 * rows * 4,
            12 * blk * rows * 4),
        name="swa_attention",
    )(q, k, k, v, v, sink_rows)


def _ffn(x, gain, w_in, w_out):
    (h,) = _rmsnorm(x, gain[None, :], BF16)
    act = _swiglu_up(h, w_in)
    return _matmul_residual(act, w_out, x, 0.5, tm=512, tn=512)


def kernel(x, ffn_norm, ffn_w_in, ffn_w_out, attn_norm, a_w_qkv, a_w_o, kv_norm,
           w_kv_shared, b_w_q, b_sinks, b_w_o, final_norm):
    b, s, d = x.shape
    depth = ffn_norm.shape[0]
    n_a = a_w_qkv.shape[0]
    a_q = a_w_o.shape[1]
    a_kv = (a_w_qkv.shape[2] - a_q) // 2
    b_kv = w_kv_shared.shape[1] // 2
    rope_a = _rope_tables(s, A_HEAD_DIM)
    rope_b = _rope_tables(s, B_HEAD_DIM)
    w_kv = w_kv_shared.astype(BF16)

    outs = []
    for bi in range(b):
        xs = x.reshape(s, d) if b == 1 else x[bi]
        k_sh = v_sh = None
        for layer in range(depth):
            if layer == n_a:
                (h_kv,) = _rmsnorm(xs, kv_norm[None, :], BF16)
                k_sh = _proj(h_kv, w_kv, 0, b_kv, rope=rope_b, half=B_HEAD_DIM // 2)
                v_sh = _proj(h_kv, w_kv, b_kv, b_kv)
            xs = _ffn(xs, ffn_norm[layer, 0], ffn_w_in[layer, 0].astype(BF16),
                      ffn_w_out[layer, 0].astype(BF16))
            (h,) = _rmsnorm(xs, attn_norm[layer][None, :], BF16)
            if layer < n_a:
                w = a_w_qkv[layer].astype(BF16)
                q = _proj(h, w, 0, a_q, rope=rope_a, half=A_HEAD_DIM // 2,
                          scale=A_HEAD_DIM ** -0.5, group_major=True)
                k = _proj(h, w, a_q, a_kv, rope=rope_a, half=A_HEAD_DIM // 2, group_major=True)
                v = _proj(h, w, a_q + a_kv, a_kv, group_major=True)
                o = _moba_attention(q, k, v)
                xs = _matmul_residual(o, a_w_o[layer].astype(BF16), xs, 1.0, tm=1024, tn=512)
            else:
                li = layer - n_a
                q = _proj(h, b_w_q[li].astype(BF16), 0, b_w_q.shape[2], rope=rope_b,
                          half=B_HEAD_DIM // 2, scale=B_HEAD_DIM ** -0.5, group_major=True)
                o = _swa_attention(q, k_sh, v_sh, b_sinks[li])
                xs = _matmul_residual(o, b_w_o[li].astype(BF16), xs, 1.0, tm=1024, tn=512)
            xs = _ffn(xs, ffn_norm[layer, 1], ffn_w_in[layer, 1].astype(BF16),
                      ffn_w_out[layer, 1].astype(BF16))
        (y,) = _rmsnorm(xs, final_norm[None, :], F32)
        outs.append(y)
    return outs[0].reshape(b, s, d) if b == 1 else jnp.stack(outs, axis=0)
```

```python
import functools

import jax
import jax.numpy as jnp
import numpy as np
from jax import lax
from jax.experimental import pallas as pl
from jax.experimental.pallas import tpu as pltpu

F32 = jnp.float32
BF16 = jnp.bfloat16

RMS_EPS = 1e-6
ROPE_THETA = 10000.0
NEG = -1e30
LOG2_E = 1.4426950408889634
TAKEN = -3e38

A_HEAD_DIM = 128
A_GROUP = 4
MOBA_BLOCK = 256
MOBA_TOPK = 3

B_HEAD_DIM = 64
B_GROUP = 8
SWA_BLOCK = 128

LANES = 128
BF16_SUBLANES = 16
UP_TILE_N = 256
V7X_VMEM_CAP = 60000 * 1024

_NT = (((1,), (1,)), ((), ()))
_TN = (((0,), (0,)), ((), ()))


def _params(semantics, block_bytes, scratch_bytes=0):
    need = 2 * block_bytes + scratch_bytes
    return pltpu.CompilerParams(
        dimension_semantics=semantics,
        vmem_limit_bytes=int(min(V7X_VMEM_CAP, max(need, 16 * 1024 * 1024))),
    )


def _rmsnorm_kernel(x_ref, g_ref, *o_refs):
    x = x_ref[...]
    y = x * lax.rsqrt(jnp.mean(x * x, axis=-1, keepdims=True) + RMS_EPS)
    for n, o_ref in enumerate(o_refs):
        o_ref[...] = (y * g_ref[n:n + 1, :]).astype(o_ref.dtype)


def _rmsnorm(x, gains, out_dtype):
    s, d = x.shape
    n = gains.shape[0]
    tm = min(512, s)
    row = pl.BlockSpec((tm, d), lambda i: (i, 0))
    outs = pl.pallas_call(
        _rmsnorm_kernel,
        grid=(s // tm,),
        in_specs=[row, pl.BlockSpec((n, d), lambda i: (0, 0))],
        out_specs=[row] * n,
        out_shape=[jax.ShapeDtypeStruct((s, d), out_dtype)] * n,
        compiler_params=_params(("parallel",), tm * d * (4 + 4 * n), tm * d * 4),
        name="rmsnorm",
    )(x, gains)
    return outs


def _swiglu_up_kernel(h_ref, wg_ref, wu_ref, o_ref):
    h = h_ref[...]
    g = jnp.dot(h, wg_ref[...], preferred_element_type=F32)
    u = jnp.dot(h, wu_ref[...], preferred_element_type=F32)
    o_ref[...] = (g * (1.0 / (1.0 + jnp.exp(-g))) * u).astype(o_ref.dtype)


def _swiglu_up(h, w_in, lead):
    s, d = h.shape
    f = w_in.shape[-1] // 2
    tm = min(1024, s)
    tn = UP_TILE_N
    nj = f // tn
    squeezed = (None,) * len(lead)
    return pl.pallas_call(
        _swiglu_up_kernel,
        grid=(s // tm, nj),
        in_specs=[
            pl.BlockSpec((tm, d), lambda i, j: (i, 0)),
            pl.BlockSpec(squeezed + (d, tn), lambda i, j: lead + (0, j)),
            pl.BlockSpec(squeezed + (d, tn), lambda i, j: lead + (0, j + nj)),
        ],
        out_specs=pl.BlockSpec((tm, tn), lambda i, j: (i, j)),
        out_shape=jax.ShapeDtypeStruct((s, f), BF16),
        compiler_params=_params(
            ("parallel", "arbitrary"),
            tm * d * 2 + 2 * d * tn * 2 + tm * tn * 2, 4 * tm * tn * 4),
        name="swiglu_up",
    )(h, w_in, w_in)


def _matmul_residual_kernel(a_ref, w_ref, r_ref, o_ref, *, alpha):
    acc = jnp.dot(a_ref[...], w_ref[...], preferred_element_type=F32)
    o_ref[...] = r_ref[...] + alpha * acc


def _matmul_residual(a, w, lead, res, alpha, tm, tn):
    s, k = a.shape
    n = w.shape[-1]
    tm, tn = min(tm, s), min(tn, n)
    squeezed = (None,) * len(lead)
    return pl.pallas_call(
        functools.partial(_matmul_residual_kernel, alpha=alpha),
        grid=(s // tm, n // tn),
        in_specs=[
            pl.BlockSpec((tm, k), lambda i, j: (i, 0)),
            pl.BlockSpec(squeezed + (k, tn), lambda i, j: lead + (0, j)),
            pl.BlockSpec((tm, tn), lambda i, j: (i, j)),
        ],
        out_specs=pl.BlockSpec((tm, tn), lambda i, j: (i, j)),
        out_shape=jax.ShapeDtypeStruct((s, n), F32),
        compiler_params=_params(
            ("parallel", "arbitrary"),
            tm * k * 2 + k * tn * 2 + 2 * tm * tn * 4, 2 * tm * tn * 4),
        name="matmul_residual",
    )(a, w, res)


def _rope_tables(seq, dim):
    half = dim // 2
    inv = 1.0 / (ROPE_THETA ** (jnp.arange(0, dim, 2, dtype=F32) / dim))
    ang = jnp.arange(seq, dtype=F32)[:, None] * inv[None, :]
    cos, sin = jnp.cos(ang), jnp.sin(ang)
    zero = jnp.zeros_like(sin)
    reps = LANES // dim
    cos_t = jnp.tile(jnp.concatenate([cos, cos], axis=-1), (1, reps))
    s_lo = jnp.tile(jnp.concatenate([-sin, zero], axis=-1), (1, reps))
    s_hi = jnp.tile(jnp.concatenate([zero, sin], axis=-1), (1, reps))
    return cos_t, s_lo, s_hi


def _proj_kernel(h_ref, w_ref, *refs, half, scale, group_major):
    acc = jnp.dot(h_ref[...], w_ref[...], preferred_element_type=F32)
    if half:
        cos_ref, slo_ref, shi_ref, o_ref = refs
        cos, s_lo, s_hi = cos_ref[...], slo_ref[...], shi_ref[...]
    else:
        (o_ref,) = refs
    for g in range(acc.shape[1] // LANES):
        x = acc[:, g * LANES:(g + 1) * LANES]
        if half:
            x = (x * cos + pltpu.roll(x, LANES - half, axis=1) * s_lo
                 + pltpu.roll(x, half, axis=1) * s_hi)
        if scale != 1.0:
            x = x * scale
        if group_major:
            o_ref[g] = x.astype(o_ref.dtype)
        else:
            o_ref[:, g * LANES:(g + 1) * LANES] = x.astype(o_ref.dtype)


def _proj(h, w, col0, ncols, *, rope=None, half=0, scale=1.0, group_major=False):
    s, k = h.shape
    tm = min(1024, s)
    tn = min(512, ncols)
    j0 = col0 // tn
    in_specs = [
        pl.BlockSpec((tm, k), lambda i, j: (i, 0)),
        pl.BlockSpec((k, tn), lambda i, j: (0, j + j0)),
    ]
    args = [h, w]
    if half:
        tab = pl.BlockSpec((tm, LANES), lambda i, j: (i, 0))
        in_specs += [tab, tab, tab]
        args += list(rope)
    if group_major:
        gpt = tn // LANES
        out_spec = pl.BlockSpec((gpt, tm, LANES), lambda i, j: (j, i, 0))
        out_shape = jax.ShapeDtypeStruct((ncols // LANES, s, LANES), BF16)
    else:
        out_spec = pl.BlockSpec((tm, tn), lambda i, j: (i, j))
        out_shape = jax.ShapeDtypeStruct((s, ncols), BF16)
    return pl.pallas_call(
        functools.partial(_proj_kernel, half=half, scale=scale, group_major=group_major),
        grid=(s // tm, ncols // tn),
        in_specs=in_specs,
        out_specs=out_spec,
        out_shape=out_shape,
        compiler_params=_params(
            ("parallel", "arbitrary"),
            tm * k * 2 + k * tn * 2 + 3 * tm * LANES * 4 + tm * tn * 2, 3 * tm * tn * 4),
        name="proj",
    )(*args)


def _moba_kernel(q_ref, k_ref, vt_ref, o_ref, kmean_ref, qa_ref, sa_ref, sb_ref, m_ref, acc_ref):
    group, blk, dh = q_ref.shape
    nb = vt_ref.shape[0]
    rows = group * blk
    i = pl.program_id(1)

    @pl.when(i == 0)
    def _block_means():
        def body(j, carry):
            kj = k_ref[pl.ds(pl.multiple_of(j * blk, blk), blk), 0:dh].astype(F32)
            kmean_ref[pl.ds(j, 1), :] = jnp.sum(kj, axis=0, keepdims=True) * (1.0 / blk)
            return carry
        lax.fori_loop(0, nb, body, 0)

    q_t = q_ref[...].reshape(rows, dh).astype(F32).T.astype(BF16)
    qa_ref[0:dh, :] = q_t

    km = kmean_ref[...]
    km_hi = km.astype(BF16)
    km_lo = (km - km_hi.astype(F32)).astype(BF16)
    gate = (jnp.dot(km_hi, q_t, preferred_element_type=F32)
            + jnp.dot(km_lo, q_t, preferred_element_type=F32))
    bidx = lax.broadcasted_iota(jnp.int32, (nb, rows), 0)
    past = bidx < i
    bidx = bidx.astype(F32)
    gate = jnp.where(past, gate, NEG)
    bias = jnp.full((nb, rows), NEG, F32)
    for _ in range(min(MOBA_TOPK, nb)):
        top = jnp.max(gate, axis=0, keepdims=True)
        first = jnp.min(jnp.where(gate == top, bidx, float(nb)), axis=0, keepdims=True)
        pick = bidx == first
        bias = jnp.where(pick & past, 0.0, bias)
        gate = jnp.where(pick, TAKEN, gate)
    qa_ref[dh:dh + nb, :] = bias.astype(BF16)
    if nb < LANES:
        qa_ref[dh + nb:dh + LANES, :] = jnp.zeros((LANES - nb, rows), BF16)

    own = pl.multiple_of(i * blk, blk)
    s = jnp.dot(k_ref[pl.ds(own, blk), 0:dh], q_t, preferred_element_type=F32)
    kpos = lax.broadcasted_iota(jnp.int32, (blk, rows), 0)
    qpos = lax.broadcasted_iota(jnp.int32, (blk, rows), 1) & (blk - 1)
    s = jnp.where(kpos <= qpos, s, NEG)
    m = jnp.max(s, axis=0, keepdims=True)
    m_ref[...] = m
    acc_ref[...] = jnp.dot(vt_ref[i], jnp.exp2(s - m).astype(BF16), preferred_element_type=F32)

    def scores(j):
        off = pl.multiple_of(j * blk, blk)
        return jnp.dot(k_ref[pl.ds(off, blk), :], qa_ref[...], preferred_element_type=F32)

    def update(s_ref, j):
        s = s_ref[...]
        m_old = m_ref[...]
        m_new = jnp.maximum(m_old, jnp.max(s, axis=0, keepdims=True))
        m_ref[...] = m_new
        p = jnp.exp2(s - m_new).astype(BF16)
        acc_ref[...] = jnp.exp2(m_old - m_new) * acc_ref[...] + jnp.dot(
            vt_ref[j], p, preferred_element_type=F32)

    sa_ref[...] = scores(0)

    def pair(t, carry):
        sb_ref[...] = scores(2 * t + 1)
        update(sa_ref, 2 * t)
        sa_ref[...] = scores(jnp.minimum(2 * t + 2, nb - 1))
        update(sb_ref, 2 * t + 1)
        return carry
    lax.fori_loop(0, (i + 1) // 2, pair, 0)

    out_t = acc_ref[0:dh, :] / acc_ref[dh:dh + 1, :]
    for g in range(group):
        o_ref[:, g * dh:(g + 1) * dh] = out_t[:, g * blk:(g + 1) * blk].T.astype(o_ref.dtype)


def _moba_attention(q, k, v):
    h, s, dh = q.shape
    hkv = k.shape[0]
    group = h // hkv
    blk = MOBA_BLOCK
    nb = s // blk
    rows = group * blk
    assert nb <= LANES and dh == LANES
    block_of_key = jnp.arange(s, dtype=jnp.int32)[:, None] // blk
    onehot = (block_of_key == jnp.arange(LANES, dtype=jnp.int32)[None, :]).astype(BF16)
    k_aug = jnp.concatenate([k, jnp.broadcast_to(onehot, (hkv, s, LANES))], axis=-1)
    v_t = v.reshape(hkv, nb, blk, dh).transpose(0, 1, 3, 2)
    ones_rows = jnp.zeros((hkv, nb, BF16_SUBLANES, blk), BF16).at[:, :, 0, :].set(1.0)
    vt_aug = jnp.concatenate([v_t, ones_rows], axis=2)
    vrows = dh + BF16_SUBLANES
    return pl.pallas_call(
        _moba_kernel,
        grid=(hkv, nb),
        in_specs=[
            pl.BlockSpec((group, blk, dh), lambda hh, i: (hh, i, 0)),
            pl.BlockSpec((None, s, 2 * LANES), lambda hh, i: (hh, 0, 0)),
            pl.BlockSpec((None, nb, vrows, blk), lambda hh, i: (hh, 0, 0, 0)),
        ],
        out_specs=pl.BlockSpec((blk, group * dh), lambda hh, i: (i, hh)),
        out_shape=jax.ShapeDtypeStruct((s, h * dh), BF16),
        scratch_shapes=[
            pltpu.VMEM((nb, dh), F32),
            pltpu.VMEM((2 * LANES, rows), BF16),
            pltpu.VMEM((blk, rows), F32),
            pltpu.VMEM((blk, rows), F32),
            pltpu.VMEM((1, rows), F32),
            pltpu.VMEM((vrows, rows), F32),
        ],
        compiler_params=_params(
            ("arbitrary", "arbitrary"),
            rows * dh * 2 + s * 2 * LANES * 2 + nb * vrows * blk * 2 + blk * group * dh * 2,
            (2 * blk + vrows + LANES + 1) * rows * 4 + 6 * blk * rows * 4),
        name="moba_attention",
    )(q, k_aug, vt_aug)


def _swa_kernel(q_ref, kp_ref, ko_ref, vp_ref, vo_ref, sink_ref, bias_ref, o_ref):
    pairs, blk, _ = q_ref.shape
    rows = pairs * blk
    odd = (pl.program_id(0) % 2) == 1

    lane = lax.broadcasted_iota(jnp.int32, (2 * blk, LANES), 1)
    lo = lane < B_HEAD_DIM

    def band(prev_ref, own_ref):
        x = jnp.concatenate([prev_ref[...], own_ref[...]], axis=0)
        swapped = pltpu.roll(x, B_HEAD_DIM, axis=1)
        mine_lo = jnp.where(odd, swapped, x)
        mine_hi = jnp.where(odd, x, swapped)
        return mine_lo, mine_hi

    k_lo, k_hi = band(kp_ref, ko_ref)
    v_lo, v_hi = band(vp_ref, vo_ref)
    zero = jnp.zeros_like(k_lo)
    k_both = jnp.where(lo, k_lo, k_hi)
    v_first = jnp.where(lo, v_lo, zero)
    v_second = jnp.where(lo, zero, v_hi)

    q = q_ref[...].reshape(rows, LANES)
    qlane = lax.broadcasted_iota(jnp.int32, (rows, LANES), 1)
    qzero = jnp.zeros_like(q)
    q_first = jnp.where(qlane < B_HEAD_DIM, q, qzero)
    q_second = jnp.where(qlane < B_HEAD_DIM, qzero, q)

    def half_head(q_half, v_half, sink):
        s = lax.dot_general(k_both, q_half, _NT, preferred_element_type=F32)
        s = s + bias_ref[...]
        m = jnp.maximum(jnp.max(s, axis=0, keepdims=True), sink)
        p = jnp.exp2(s - m)
        denom = jnp.sum(p, axis=0, keepdims=True) + jnp.exp2(sink - m)
        o_t = lax.dot_general(v_half, p.astype(BF16), _TN, preferred_element_type=F32)
        return o_t / denom

    out_t = (half_head(q_first, v_first, sink_ref[0:1, :])
             + half_head(q_second, v_second, sink_ref[1:2, :]))
    for pr in range(pairs):
        o_ref[:, pr * LANES:(pr + 1) * LANES] = (
            out_t[:, pr * blk:(pr + 1) * blk].T.astype(o_ref.dtype))


def _swa_attention(q, k, v, sinks):
    npair, s, _ = q.shape
    hkv = k.shape[1] // B_HEAD_DIM
    pairs = npair // hkv
    blk = SWA_BLOCK
    nblk = s // blk
    rows = pairs * blk
    sink_rows = jnp.repeat(
        (sinks.astype(F32) * LOG2_E).reshape(hkv, pairs, 2).transpose(0, 2, 1), blk, axis=-1)
    kk = np.arange(2 * blk)[:, None]
    tt = np.arange(rows)[None, :] % blk
    in_window = (kk > tt) & (kk <= tt + blk)
    band_bias = jnp.asarray(
        np.where(np.stack([in_window & (kk >= blk), in_window]), 0.0, NEG), dtype=F32)
    prev = pl.BlockSpec((blk, LANES), lambda hh, n: (jnp.maximum(n - 1, 0), hh // 2))
    own = pl.BlockSpec((blk, LANES), lambda hh, n: (n, hh // 2))
    return pl.pallas_call(
        _swa_kernel,
        grid=(hkv, nblk),
        in_specs=[
            pl.BlockSpec((pairs, blk, LANES), lambda hh, n: (hh, n, 0)),
            prev, own, prev, own,
            pl.BlockSpec((None, 2, rows), lambda hh, n: (hh, 0, 0)),
            pl.BlockSpec((None, 2 * blk, rows), lambda hh, n: (jnp.minimum(n, 1), 0, 0)),
        ],
        out_specs=pl.BlockSpec((blk, pairs * LANES), lambda hh, n: (n, hh)),
        out_shape=jax.ShapeDtypeStruct((s, npair * LANES), BF16),
        compiler_params=_params(
            ("parallel", "parallel"),
            2 * rows * LANES * 2 + 4 * blk * LANES * 2 + 2 * rows * 4 + 2 * blk * rows * 4,
            12 * blk * rows * 4),
        name="swa_attention",
    )(q, k, k, v, v, sink_rows, band_bias)


def _ffn(x, gain, w_in, w_out, lead):
    (h,) = _rmsnorm(x, gain[None, :], BF16)
    act = _swiglu_up(h, w_in, lead)
    return _matmul_residual(act, w_out, lead, x, 0.5, tm=512, tn=512)


def kernel(x, ffn_norm, ffn_w_in, ffn_w_out, attn_norm, a_w_qkv, a_w_o, kv_norm,
           w_kv_shared, b_w_q, b_sinks, b_w_o, final_norm):
    b, s, d = x.shape
    depth = ffn_norm.shape[0]
    n_a = a_w_qkv.shape[0]
    a_q = a_w_o.shape[1]
    a_kv = (a_w_qkv.shape[2] - a_q) // 2
    b_kv = w_kv_shared.shape[1] // 2
    rope_a = _rope_tables(s, A_HEAD_DIM)
    rope_b = _rope_tables(s, B_HEAD_DIM)
    w_in, w_out = ffn_w_in.astype(BF16), ffn_w_out.astype(BF16)
    w_qkv, w_ao = a_w_qkv.astype(BF16), a_w_o.astype(BF16)
    w_kv, w_bq, w_bo = w_kv_shared.astype(BF16), b_w_q.astype(BF16), b_w_o.astype(BF16)

    outs = []
    for bi in range(b):
        xs = x.reshape(s, d) if b == 1 else x[bi]
        k_sh = v_sh = None
        for layer in range(depth):
            if layer == n_a:
                (h_kv,) = _rmsnorm(xs, kv_norm[None, :], BF16)
                k_sh = _proj(h_kv, w_kv, 0, b_kv, rope=rope_b, half=B_HEAD_DIM // 2)
                v_sh = _proj(h_kv, w_kv, b_kv, b_kv)
            xs = _ffn(xs, ffn_norm[layer, 0], w_in, w_out, (layer, 0))
            (h,) = _rmsnorm(xs, attn_norm[layer][None, :], BF16)
            if layer < n_a:
                w = w_qkv[layer]
                q = _proj(h, w, 0, a_q, rope=rope_a, half=A_HEAD_DIM // 2,
                          scale=A_HEAD_DIM ** -0.5 * LOG2_E, group_major=True)
                k = _proj(h, w, a_q, a_kv, rope=rope_a, half=A_HEAD_DIM // 2, group_major=True)
                v = _proj(h, w, a_q + a_kv, a_kv, group_major=True)
                o = _moba_attention(q, k, v)
                xs = _matmul_residual(o, w_ao, (layer,), xs, 1.0, tm=1024, tn=512)
            else:
                li = layer - n_a
                q = _proj(h, w_bq[li], 0, w_bq.shape[2], rope=rope_b,
                          half=B_HEAD_DIM // 2, scale=B_HEAD_DIM ** -0.5 * LOG2_E, group_major=True)
                o = _swa_attention(q, k_sh, v_sh, b_sinks[li])
                xs = _matmul_residual(o, w_bo, (li,), xs, 1.0, tm=1024, tn=512)
            xs = _ffn(xs, ffn_norm[layer, 1], w_in, w_out, (layer, 1))
        (y,) = _rmsnorm(xs, final_norm[None, :], F32)
        outs.append(y)
    return outs[0].reshape(b, s, d) if b == 1 else jnp.stack(outs, axis=0)
```

```python
import functools

import jax
import jax.numpy as jnp
import numpy as np
from jax import lax
from jax.experimental import pallas as pl
from jax.experimental.pallas import tpu as pltpu

F32 = jnp.float32
BF16 = jnp.bfloat16

RMS_EPS = 1e-6
ROPE_THETA = 10000.0
NEG = -1e30
LOG2_E = 1.4426950408889634
TAKEN = -3e38

A_HEAD_DIM = 128
A_GROUP = 4
MOBA_BLOCK = 256
MOBA_TOPK = 3

B_HEAD_DIM = 64
B_GROUP = 8
SWA_BLOCK = 128
SWA_STEP_BLOCKS = 4

LANES = 128
MXU_COLS = 256
BF16_SUBLANES = 16
UP_TILE_N = 256
V7X_VMEM_CAP = 60000 * 1024

_NT = (((1,), (1,)), ((), ()))
_TN = (((0,), (0,)), ((), ()))


def _params(semantics, block_bytes, scratch_bytes=0):
    need = 2 * block_bytes + scratch_bytes
    return pltpu.CompilerParams(
        dimension_semantics=semantics,
        vmem_limit_bytes=int(min(V7X_VMEM_CAP, max(need, 16 * 1024 * 1024))),
    )


def _rmsnorm_kernel(x_ref, g_ref, *o_refs):
    x = x_ref[...]
    y = x * lax.rsqrt(jnp.mean(x * x, axis=-1, keepdims=True) + RMS_EPS)
    for n, o_ref in enumerate(o_refs):
        o_ref[...] = (y * g_ref[n:n + 1, :]).astype(o_ref.dtype)


def _rmsnorm(x, gains, out_dtype):
    s, d = x.shape
    n = gains.shape[0]
    tm = min(512, s)
    row = pl.BlockSpec((tm, d), lambda i: (i, 0))
    outs = pl.pallas_call(
        _rmsnorm_kernel,
        grid=(s // tm,),
        in_specs=[row, pl.BlockSpec((n, d), lambda i: (0, 0))],
        out_specs=[row] * n,
        out_shape=[jax.ShapeDtypeStruct((s, d), out_dtype)] * n,
        compiler_params=_params(("parallel",), tm * d * (4 + 4 * n), tm * d * 4),
        name="rmsnorm",
    )(x, gains)
    return outs


def _swiglu_up_kernel(h_ref, wg_ref, wu_ref, o_ref):
    h = h_ref[...]
    g = jnp.dot(h, wg_ref[...], preferred_element_type=F32)
    u = jnp.dot(h, wu_ref[...], preferred_element_type=F32)
    o_ref[...] = (g * (1.0 / (1.0 + jnp.exp(-g))) * u).astype(o_ref.dtype)


def _swiglu_up(h, w_in, lead):
    s, d = h.shape
    f = w_in.shape[-1] // 2
    tm = min(1024, s)
    tn = UP_TILE_N
    nj = f // tn
    squeezed = (None,) * len(lead)
    return pl.pallas_call(
        _swiglu_up_kernel,
        grid=(s // tm, nj),
        in_specs=[
            pl.BlockSpec((tm, d), lambda i, j: (i, 0)),
            pl.BlockSpec(squeezed + (d, tn), lambda i, j: lead + (0, j)),
            pl.BlockSpec(squeezed + (d, tn), lambda i, j: lead + (0, j + nj)),
        ],
        out_specs=pl.BlockSpec((tm, tn), lambda i, j: (i, j)),
        out_shape=jax.ShapeDtypeStruct((s, f), BF16),
        compiler_params=_params(
            ("parallel", "arbitrary"),
            tm * d * 2 + 2 * d * tn * 2 + tm * tn * 2, 4 * tm * tn * 4),
        name="swiglu_up",
    )(h, w_in, w_in)


def _matmul_residual_kernel(a_ref, w_ref, r_ref, o_ref, *, alpha):
    acc = jnp.dot(a_ref[...], w_ref[...], preferred_element_type=F32)
    o_ref[...] = r_ref[...] + alpha * acc


def _matmul_residual(a, w, lead, res, alpha, tm, tn):
    s, k = a.shape
    n = w.shape[-1]
    tm, tn = min(tm, s), min(tn, n)
    squeezed = (None,) * len(lead)
    return pl.pallas_call(
        functools.partial(_matmul_residual_kernel, alpha=alpha),
        grid=(s // tm, n // tn),
        in_specs=[
            pl.BlockSpec((tm, k), lambda i, j: (i, 0)),
            pl.BlockSpec(squeezed + (k, tn), lambda i, j: lead + (0, j)),
            pl.BlockSpec((tm, tn), lambda i, j: (i, j)),
        ],
        out_specs=pl.BlockSpec((tm, tn), lambda i, j: (i, j)),
        out_shape=jax.ShapeDtypeStruct((s, n), F32),
        compiler_params=_params(
            ("parallel", "arbitrary"),
            tm * k * 2 + k * tn * 2 + 2 * tm * tn * 4, 2 * tm * tn * 4),
        name="matmul_residual",
    )(a, w, res)


def _rope_tables(seq, dim):
    half = dim // 2
    inv = 1.0 / (ROPE_THETA ** (jnp.arange(0, dim, 2, dtype=F32) / dim))
    ang = jnp.arange(seq, dtype=F32)[:, None] * inv[None, :]
    cos, sin = jnp.cos(ang), jnp.sin(ang)
    zero = jnp.zeros_like(sin)
    reps = LANES // dim
    cos_t = jnp.tile(jnp.concatenate([cos, cos], axis=-1), (1, reps))
    s_lo = jnp.tile(jnp.concatenate([-sin, zero], axis=-1), (1, reps))
    s_hi = jnp.tile(jnp.concatenate([zero, sin], axis=-1), (1, reps))
    return cos_t, s_lo, s_hi


def _proj_kernel(h_ref, w_ref, *refs, half, scale, group_major):
    acc = jnp.dot(h_ref[...], w_ref[...], preferred_element_type=F32)
    if half:
        cos_ref, slo_ref, shi_ref, o_ref = refs
        cos, s_lo, s_hi = cos_ref[...], slo_ref[...], shi_ref[...]
    else:
        (o_ref,) = refs
    for g in range(acc.shape[1] // LANES):
        x = acc[:, g * LANES:(g + 1) * LANES]
        if half:
            x = (x * cos + pltpu.roll(x, LANES - half, axis=1) * s_lo
                 + pltpu.roll(x, half, axis=1) * s_hi)
        if scale != 1.0:
            x = x * scale
        if group_major:
            o_ref[g] = x.astype(o_ref.dtype)
        else:
            o_ref[:, g * LANES:(g + 1) * LANES] = x.astype(o_ref.dtype)


def _proj(h, w, col0, ncols, *, rope=None, half=0, scale=1.0, group_major=False):
    s, k = h.shape
    tm = min(1024, s)
    tn = min(512, ncols)
    j0 = col0 // tn
    in_specs = [
        pl.BlockSpec((tm, k), lambda i, j: (i, 0)),
        pl.BlockSpec((k, tn), lambda i, j: (0, j + j0)),
    ]
    args = [h, w]
    if half:
        tab = pl.BlockSpec((tm, LANES), lambda i, j: (i, 0))
        in_specs += [tab, tab, tab]
        args += list(rope)
    if group_major:
        gpt = tn // LANES
        out_spec = pl.BlockSpec((gpt, tm, LANES), lambda i, j: (j, i, 0))
        out_shape = jax.ShapeDtypeStruct((ncols // LANES, s, LANES), BF16)
    else:
        out_spec = pl.BlockSpec((tm, tn), lambda i, j: (i, j))
        out_shape = jax.ShapeDtypeStruct((s, ncols), BF16)
    return pl.pallas_call(
        functools.partial(_proj_kernel, half=half, scale=scale, group_major=group_major),
        grid=(s // tm, ncols // tn),
        in_specs=in_specs,
        out_specs=out_spec,
        out_shape=out_shape,
        compiler_params=_params(
            ("parallel", "arbitrary"),
            tm * k * 2 + k * tn * 2 + 3 * tm * LANES * 4 + tm * tn * 2, 3 * tm * tn * 4),
        name="proj",
    )(*args)


def _moba_kernel(q_ref, k_ref, vt_ref, o_ref, kmean_ref, qa_ref, sa_ref, sb_ref, m_ref, acc_ref):
    group, blk, dh = q_ref.shape
    nb = vt_ref.shape[0]
    rows = group * blk
    i = pl.program_id(1)

    @pl.when(i == 0)
    def _block_means():
        def body(j, carry):
            kj = k_ref[pl.ds(pl.multiple_of(j * blk, blk), blk), 0:dh].astype(F32)
            kmean_ref[pl.ds(j, 1), :] = jnp.sum(kj, axis=0, keepdims=True) * (1.0 / blk)
            return carry
        lax.fori_loop(0, nb, body, 0)

    q_t = q_ref[...].reshape(rows, dh).astype(F32).T.astype(BF16)
    qa_ref[0:dh, :] = q_t

    km = kmean_ref[...]
    km_hi = km.astype(BF16)
    km_lo = (km - km_hi.astype(F32)).astype(BF16)
    gate = (jnp.dot(km_hi, q_t, preferred_element_type=F32)
            + jnp.dot(km_lo, q_t, preferred_element_type=F32))
    bidx = lax.broadcasted_iota(jnp.int32, (nb, rows), 0)
    past = bidx < i
    bidx = bidx.astype(F32)
    gate = jnp.where(past, gate, NEG)
    bias = jnp.full((nb, rows), NEG, F32)
    for _ in range(min(MOBA_TOPK, nb)):
        top = jnp.max(gate, axis=0, keepdims=True)
        first = jnp.min(jnp.where(gate == top, bidx, float(nb)), axis=0, keepdims=True)
        pick = bidx == first
        bias = jnp.where(pick & past, 0.0, bias)
        gate = jnp.where(pick, TAKEN, gate)
    qa_ref[dh:dh + nb, :] = bias.astype(BF16)
    if nb < LANES:
        qa_ref[dh + nb:dh + LANES, :] = jnp.zeros((LANES - nb, rows), BF16)

    own = pl.multiple_of(i * blk, blk)
    s = jnp.dot(k_ref[pl.ds(own, blk), 0:dh], q_t, preferred_element_type=F32)
    kpos = lax.broadcasted_iota(jnp.int32, (blk, rows), 0)
    qpos = lax.broadcasted_iota(jnp.int32, (blk, rows), 1) & (blk - 1)
    s = jnp.where(kpos <= qpos, s, NEG)
    m = jnp.max(s, axis=0, keepdims=True)
    m_ref[...] = m
    acc_ref[...] = jnp.dot(vt_ref[i], jnp.exp2(s - m).astype(BF16), preferred_element_type=F32)

    def scores(j, s_ref):
        off = pl.multiple_of(jnp.minimum(j, nb - 1) * blk, blk)
        s_ref[...] = jnp.dot(k_ref[pl.ds(off, blk), :], qa_ref[...], preferred_element_type=F32)

    def update(s_ref, j):
        v_t = vt_ref[jnp.minimum(j, nb - 1)]
        for c in range(rows // MXU_COLS):
            cols = slice(c * MXU_COLS, (c + 1) * MXU_COLS)
            s = s_ref[:, cols]
            m_old = m_ref[:, cols]
            m_new = jnp.maximum(m_old, jnp.max(s, axis=0, keepdims=True))
            m_ref[:, cols] = m_new
            p = jnp.exp2(s - m_new).astype(BF16)
            acc_ref[:, cols] = jnp.exp2(m_old - m_new) * acc_ref[:, cols] + jnp.dot(
                v_t, p, preferred_element_type=F32)

    scores(0, sa_ref)

    def quad(t, carry):
        scores(4 * t + 1, sb_ref)
        update(sa_ref, 4 * t)
        scores(4 * t + 2, sa_ref)
        update(sb_ref, 4 * t + 1)
        scores(4 * t + 3, sb_ref)
        update(sa_ref, 4 * t + 2)
        scores(4 * t + 4, sa_ref)
        update(sb_ref, 4 * t + 3)
        return carry
    lax.fori_loop(0, (i + 3) // 4, quad, 0)

    out_t = acc_ref[0:dh, :] / acc_ref[dh:dh + 1, :]
    for g in range(group):
        o_ref[:, g * dh:(g + 1) * dh] = out_t[:, g * blk:(g + 1) * blk].T.astype(o_ref.dtype)


def _moba_attention(q, k, v):
    h, s, dh = q.shape
    hkv = k.shape[0]
    group = h // hkv
    blk = MOBA_BLOCK
    nb = s // blk
    rows = group * blk
    assert nb <= LANES and dh == LANES
    block_of_key = jnp.arange(s, dtype=jnp.int32)[:, None] // blk
    onehot = (block_of_key == jnp.arange(LANES, dtype=jnp.int32)[None, :]).astype(BF16)
    k_aug = jnp.concatenate([k, jnp.broadcast_to(onehot, (hkv, s, LANES))], axis=-1)
    v_t = v.reshape(hkv, nb, blk, dh).transpose(0, 1, 3, 2)
    ones_rows = jnp.zeros((hkv, nb, BF16_SUBLANES, blk), BF16).at[:, :, 0, :].set(1.0)
    vt_aug = jnp.concatenate([v_t, ones_rows], axis=2)
    vrows = dh + BF16_SUBLANES
    return pl.pallas_call(
        _moba_kernel,
        grid=(hkv, nb),
        in_specs=[
            pl.BlockSpec((group, blk, dh), lambda hh, i: (hh, i, 0)),
            pl.BlockSpec((None, s, 2 * LANES), lambda hh, i: (hh, 0, 0)),
            pl.BlockSpec((None, nb, vrows, blk), lambda hh, i: (hh, 0, 0, 0)),
        ],
        out_specs=pl.BlockSpec((blk, group * dh), lambda hh, i: (i, hh)),
        out_shape=jax.ShapeDtypeStruct((s, h * dh), BF16),
        scratch_shapes=[
            pltpu.VMEM((nb, dh), F32),
            pltpu.VMEM((2 * LANES, rows), BF16),
            pltpu.VMEM((blk, rows), F32),
            pltpu.VMEM((blk, rows), F32),
            pltpu.VMEM((1, rows), F32),
            pltpu.VMEM((vrows, rows), F32),
        ],
        compiler_params=_params(
            ("arbitrary", "arbitrary"),
            rows * dh * 2 + s * 2 * LANES * 2 + nb * vrows * blk * 2 + blk * group * dh * 2,
            (2 * blk + vrows + LANES + 1) * rows * 4 + 6 * blk * rows * 4),
        name="moba_attention",
    )(q, k_aug, vt_aug)


def _swa_kernel(q_ref, kp_ref, ko_ref, vp_ref, vo_ref, sink_ref, bias_ref, o_ref, *, nq):
    pairs, span, _ = q_ref.shape
    blk = span // nq
    rows = pairs * blk
    odd = (pl.program_id(0) % 2) == 1
    first_step = pl.program_id(1) == 0

    lane = lax.broadcasted_iota(jnp.int32, (span + blk, LANES), 1)
    lo = lane < B_HEAD_DIM

    def strip(prev_ref, own_ref):
        x = jnp.concatenate([prev_ref[...], own_ref[...]], axis=0)
        swapped = pltpu.roll(x, B_HEAD_DIM, axis=1)
        mine_lo = jnp.where(odd, swapped, x)
        mine_hi = jnp.where(odd, x, swapped)
        return mine_lo, mine_hi

    k_lo, k_hi = strip(kp_ref, ko_ref)
    v_lo, v_hi = strip(vp_ref, vo_ref)
    zero = jnp.zeros_like(k_lo)
    k_both = jnp.where(lo, k_lo, k_hi)
    v_first = jnp.where(lo, v_lo, zero)
    v_second = jnp.where(lo, zero, v_hi)

    qlane = lax.broadcasted_iota(jnp.int32, (rows, LANES), 1)
    q_is_first = qlane < B_HEAD_DIM

    for u in range(nq):
        band = slice(u * blk, (u + 2) * blk)
        q = q_ref[:, u * blk:(u + 1) * blk, :].reshape(rows, LANES)
        qzero = jnp.zeros_like(q)
        bias = bias_ref[1] if u else bias_ref[jnp.where(first_step, 0, 1)]

        def half_head(q_half, v_half, sink):
            s = lax.dot_general(k_both[band], q_half, _NT, preferred_element_type=F32)
            s = s + bias
            m = jnp.maximum(jnp.max(s, axis=0, keepdims=True), sink)
            p = jnp.exp2(s - m)
            denom = jnp.sum(p, axis=0, keepdims=True) + jnp.exp2(sink - m)
            o_t = lax.dot_general(v_half[band], p.astype(BF16), _TN, preferred_element_type=F32)
            return o_t / denom

        out_t = (half_head(jnp.where(q_is_first, q, qzero), v_first, sink_ref[0:1, :])
                 + half_head(jnp.where(q_is_first, qzero, q), v_second, sink_ref[1:2, :]))
        for pr in range(pairs):
            o_ref[u * blk:(u + 1) * blk, pr * LANES:(pr + 1) * LANES] = (
                out_t[:, pr * blk:(pr + 1) * blk].T.astype(o_ref.dtype))


def _swa_attention(q, k, v, sinks):
    npair, s, _ = q.shape
    hkv = k.shape[1] // B_HEAD_DIM
    pairs = npair // hkv
    blk = SWA_BLOCK
    nq = SWA_STEP_BLOCKS
    span = nq * blk
    rows = pairs * blk
    sink_rows = jnp.repeat(
        (sinks.astype(F32) * LOG2_E).reshape(hkv, pairs, 2).transpose(0, 2, 1), blk, axis=-1)
    kk = np.arange(2 * blk)[:, None]
    tt = np.arange(rows)[None, :] % blk
    in_window = (kk > tt) & (kk <= tt + blk)
    band_bias = jnp.asarray(
        np.where(np.stack([in_window & (kk >= blk), in_window]), 0.0, NEG), dtype=F32)
    prev = pl.BlockSpec((blk, LANES), lambda hh, n: (jnp.maximum(n * nq - 1, 0), hh // 2))
    own = pl.BlockSpec((span, LANES), lambda hh, n: (n, hh // 2))
    return pl.pallas_call(
        functools.partial(_swa_kernel, nq=nq),
        grid=(hkv, s // span),
        in_specs=[
            pl.BlockSpec((pairs, span, LANES), lambda hh, n: (hh, n, 0)),
            prev, own, prev, own,
            pl.BlockSpec((None, 2, rows), lambda hh, n: (hh, 0, 0)),
            pl.BlockSpec((2, 2 * blk, rows), lambda hh, n: (0, 0, 0)),
        ],
        out_specs=pl.BlockSpec((span, pairs * LANES), lambda hh, n: (n, hh)),
        out_shape=jax.ShapeDtypeStruct((s, npair * LANES), BF16),
        compiler_params=_params(
            ("parallel", "parallel"),
            2 * pairs * span * LANES * 2 + 2 * (span + blk) * LANES * 2 + 2 * rows * 4
            + 4 * blk * rows * 4,
            12 * blk * rows * 4),
        name="swa_attention",
    )(q, k, k, v, v, sink_rows, band_bias)


def _ffn(x, gain, w_in, w_out, lead):
    (h,) = _rmsnorm(x, gain[None, :], BF16)
    act = _swiglu_up(h, w_in, lead)
    return _matmul_residual(act, w_out, lead, x, 0.5, tm=512, tn=512)


def kernel(x, ffn_norm, ffn_w_in, ffn_w_out, attn_norm, a_w_qkv, a_w_o, kv_norm,
           w_kv_shared, b_w_q, b_sinks, b_w_o, final_norm):
    b, s, d = x.shape
    depth = ffn_norm.shape[0]
    n_a = a_w_qkv.shape[0]
    a_q = a_w_o.shape[1]
    a_kv = (a_w_qkv.shape[2] - a_q) // 2
    b_kv = w_kv_shared.shape[1] // 2
    rope_a = _rope_tables(s, A_HEAD_DIM)
    rope_b = _rope_tables(s, B_HEAD_DIM)
    w_in, w_out = ffn_w_in.astype(BF16), ffn_w_out.astype(BF16)
    w_qkv, w_ao = a_w_qkv.astype(BF16), a_w_o.astype(BF16)
    w_kv, w_bq, w_bo = w_kv_shared.astype(BF16), b_w_q.astype(BF16), b_w_o.astype(BF16)

    outs = []
    for bi in range(b):
        xs = x.reshape(s, d) if b == 1 else x[bi]
        k_sh = v_sh = None
        for layer in range(depth):
            if layer == n_a:
                (h_kv,) = _rmsnorm(xs, kv_norm[None, :], BF16)
                k_sh = _proj(h_kv, w_kv, 0, b_kv, rope=rope_b, half=B_HEAD_DIM // 2)
                v_sh = _proj(h_kv, w_kv, b_kv, b_kv)
            xs = _ffn(xs, ffn_norm[layer, 0], w_in, w_out, (layer, 0))
            (h,) = _rmsnorm(xs, attn_norm[layer][None, :], BF16)
            if layer < n_a:
                w = w_qkv[layer]
                q = _proj(h, w, 0, a_q, rope=rope_a, half=A_HEAD_DIM // 2,
                          scale=A_HEAD_DIM ** -0.5 * LOG2_E, group_major=True)
                k = _proj(h, w, a_q, a_kv, rope=rope_a, half=A_HEAD_DIM // 2, group_major=True)
                v = _proj(h, w, a_q + a_kv, a_kv, group_major=True)
                o = _moba_attention(q, k, v)
                xs = _matmul_residual(o, w_ao, (layer,), xs, 1.0, tm=1024, tn=512)
            else:
                li = layer - n_a
                q = _proj(h, w_bq[li], 0, w_bq.shape[2], rope=rope_b,
                          half=B_HEAD_DIM // 2, scale=B_HEAD_DIM ** -0.5 * LOG2_E, group_major=True)
                o = _swa_attention(q, k_sh, v_sh, b_sinks[li])
                xs = _matmul_residual(o, w_bo, (li,), xs, 1.0, tm=1024, tn=512)
            xs = _ffn(xs, ffn_norm[layer, 1], w_in, w_out, (layer, 1))
        (y,) = _rmsnorm(xs, final_norm[None, :], F32)
        outs.append(y)
    return outs[0].reshape(b, s, d) if b == 1 else jnp.stack(outs, axis=0)
```

```python
import functools

import jax
import jax.numpy as jnp
import numpy as np
from jax import lax
from jax.experimental import pallas as pl
from jax.experimental.pallas import tpu as pltpu

F32 = jnp.float32
BF16 = jnp.bfloat16

RMS_EPS = 1e-6
ROPE_THETA = 10000.0
NEG = -1e30
LOG2_E = 1.4426950408889634
TAKEN = -3e38

A_HEAD_DIM = 128
A_GROUP = 4
MOBA_BLOCK = 256
MOBA_TOPK = 3

B_HEAD_DIM = 64
B_GROUP = 8
SWA_BLOCK = 128
SWA_STEP_BLOCKS = 4

LANES = 128
MXU_COLS = 256
BF16_SUBLANES = 16
UP_TILE_N = 256
V7X_VMEM_CAP = 60000 * 1024

_NT = (((1,), (1,)), ((), ()))
_TN = (((0,), (0,)), ((), ()))


def _params(semantics, block_bytes, scratch_bytes=0):
    need = 2 * block_bytes + scratch_bytes
    return pltpu.CompilerParams(
        dimension_semantics=semantics,
        vmem_limit_bytes=int(min(V7X_VMEM_CAP, max(need, 16 * 1024 * 1024))),
    )


def _rmsnorm_kernel(x_ref, g_ref, *o_refs):
    x = x_ref[...]
    y = x * lax.rsqrt(jnp.mean(x * x, axis=-1, keepdims=True) + RMS_EPS)
    for n, o_ref in enumerate(o_refs):
        o_ref[...] = (y * g_ref[n:n + 1, :]).astype(o_ref.dtype)


def _rmsnorm(x, gains, out_dtype):
    s, d = x.shape
    n = gains.shape[0]
    tm = min(512, s)
    row = pl.BlockSpec((tm, d), lambda i: (i, 0))
    outs = pl.pallas_call(
        _rmsnorm_kernel,
        grid=(s // tm,),
        in_specs=[row, pl.BlockSpec((n, d), lambda i: (0, 0))],
        out_specs=[row] * n,
        out_shape=[jax.ShapeDtypeStruct((s, d), out_dtype)] * n,
        compiler_params=_params(("parallel",), tm * d * (4 + 4 * n), tm * d * 4),
        name="rmsnorm",
    )(x, gains)
    return outs


def _swiglu_up_kernel(h_ref, wg_ref, wu_ref, o_ref):
    h = h_ref[...]
    g = jnp.dot(h, wg_ref[...].astype(BF16), preferred_element_type=F32)
    u = jnp.dot(h, wu_ref[...].astype(BF16), preferred_element_type=F32)
    o_ref[...] = (g * (1.0 / (1.0 + jnp.exp(-g))) * u).astype(o_ref.dtype)


def _swiglu_up(h, w_in, lead):
    s, d = h.shape
    f = w_in.shape[-1] // 2
    tm = min(1024, s)
    tn = UP_TILE_N
    nj = f // tn
    squeezed = (None,) * len(lead)
    return pl.pallas_call(
        _swiglu_up_kernel,
        grid=(s // tm, nj),
        in_specs=[
            pl.BlockSpec((tm, d), lambda i, j: (i, 0)),
            pl.BlockSpec(squeezed + (d, tn), lambda i, j: lead + (0, j)),
            pl.BlockSpec(squeezed + (d, tn), lambda i, j: lead + (0, j + nj)),
        ],
        out_specs=pl.BlockSpec((tm, tn), lambda i, j: (i, j)),
        out_shape=jax.ShapeDtypeStruct((s, f), BF16),
        compiler_params=_params(
            ("parallel", "arbitrary"),
            tm * d * 2 + 2 * d * tn * w_in.dtype.itemsize + tm * tn * 2,
            4 * tm * tn * 4 + 2 * d * tn * 2),
        name="swiglu_up",
    )(h, w_in, w_in)


def _matmul_residual_kernel(a_ref, w_ref, r_ref, o_ref, *, alpha):
    acc = jnp.dot(a_ref[...], w_ref[...].astype(BF16), preferred_element_type=F32)
    o_ref[...] = r_ref[...] + alpha * acc


def _matmul_residual(a, w, lead, res, alpha, tm, tn):
    s, k = a.shape
    n = w.shape[-1]
    tm, tn = min(tm, s), min(tn, n)
    squeezed = (None,) * len(lead)
    return pl.pallas_call(
        functools.partial(_matmul_residual_kernel, alpha=alpha),
        grid=(s // tm, n // tn),
        in_specs=[
            pl.BlockSpec((tm, k), lambda i, j: (i, 0)),
            pl.BlockSpec(squeezed + (k, tn), lambda i, j: lead + (0, j)),
            pl.BlockSpec((tm, tn), lambda i, j: (i, j)),
        ],
        out_specs=pl.BlockSpec((tm, tn), lambda i, j: (i, j)),
        out_shape=jax.ShapeDtypeStruct((s, n), F32),
        compiler_params=_params(
            ("parallel", "arbitrary"),
            tm * k * 2 + k * tn * w.dtype.itemsize + 2 * tm * tn * 4,
            2 * tm * tn * 4 + k * tn * 2),
        name="matmul_residual",
    )(a, w, res)


def _rope_tables(seq, dim):
    half = dim // 2
    inv = 1.0 / (ROPE_THETA ** (jnp.arange(0, dim, 2, dtype=F32) / dim))
    ang = jnp.arange(seq, dtype=F32)[:, None] * inv[None, :]
    cos, sin = jnp.cos(ang), jnp.sin(ang)
    zero = jnp.zeros_like(sin)
    reps = LANES // dim
    cos_t = jnp.tile(jnp.concatenate([cos, cos], axis=-1), (1, reps))
    s_lo = jnp.tile(jnp.concatenate([-sin, zero], axis=-1), (1, reps))
    s_hi = jnp.tile(jnp.concatenate([zero, sin], axis=-1), (1, reps))
    return cos_t, s_lo, s_hi


def _proj_kernel(h_ref, w_ref, *refs, half, scale, group_major):
    acc = jnp.dot(h_ref[...], w_ref[...].astype(BF16), preferred_element_type=F32)
    if half:
        cos_ref, slo_ref, shi_ref, o_ref = refs
        cos, s_lo, s_hi = cos_ref[...], slo_ref[...], shi_ref[...]
    else:
        (o_ref,) = refs
    for g in range(acc.shape[1] // LANES):
        x = acc[:, g * LANES:(g + 1) * LANES]
        if half:
            x = (x * cos + pltpu.roll(x, LANES - half, axis=1) * s_lo
                 + pltpu.roll(x, half, axis=1) * s_hi)
        if scale != 1.0:
            x = x * scale
        if group_major:
            o_ref[g] = x.astype(o_ref.dtype)
        else:
            o_ref[:, g * LANES:(g + 1) * LANES] = x.astype(o_ref.dtype)


def _proj(h, w, col0, ncols, *, rope=None, half=0, scale=1.0, group_major=False):
    s, k = h.shape
    tm = min(1024, s)
    tn = min(512, ncols)
    j0 = col0 // tn
    in_specs = [
        pl.BlockSpec((tm, k), lambda i, j: (i, 0)),
        pl.BlockSpec((k, tn), lambda i, j: (0, j + j0)),
    ]
    args = [h, w]
    if half:
        tab = pl.BlockSpec((tm, LANES), lambda i, j: (i, 0))
        in_specs += [tab, tab, tab]
        args += list(rope)
    if group_major:
        gpt = tn // LANES
        out_spec = pl.BlockSpec((gpt, tm, LANES), lambda i, j: (j, i, 0))
        out_shape = jax.ShapeDtypeStruct((ncols // LANES, s, LANES), BF16)
    else:
        out_spec = pl.BlockSpec((tm, tn), lambda i, j: (i, j))
        out_shape = jax.ShapeDtypeStruct((s, ncols), BF16)
    return pl.pallas_call(
        functools.partial(_proj_kernel, half=half, scale=scale, group_major=group_major),
        grid=(s // tm, ncols // tn),
        in_specs=in_specs,
        out_specs=out_spec,
        out_shape=out_shape,
        compiler_params=_params(
            ("parallel", "arbitrary"),
            tm * k * 2 + k * tn * w.dtype.itemsize + 3 * tm * LANES * 4 + tm * tn * 2,
            3 * tm * tn * 4 + k * tn * 2),
        name="proj",
    )(*args)


def _moba_kernel(q_ref, k_ref, vt_ref, o_ref, kmean_ref, qa_ref, sa_ref, sb_ref, m_ref, acc_ref):
    group, blk, dh = q_ref.shape
    nb = vt_ref.shape[0]
    rows = group * blk
    i = pl.program_id(1)

    @pl.when(i == 0)
    def _block_means():
        def body(j, carry):
            kj = k_ref[pl.ds(pl.multiple_of(j * blk, blk), blk), 0:dh].astype(F32)
            kmean_ref[pl.ds(j, 1), :] = jnp.sum(kj, axis=0, keepdims=True) * (1.0 / blk)
            return carry
        lax.fori_loop(0, nb, body, 0)

    q_t = q_ref[...].reshape(rows, dh).astype(F32).T.astype(BF16)
    qa_ref[0:dh, :] = q_t

    km = kmean_ref[...]
    km_hi = km.astype(BF16)
    km_lo = (km - km_hi.astype(F32)).astype(BF16)
    gate = (jnp.dot(km_hi, q_t, preferred_element_type=F32)
            + jnp.dot(km_lo, q_t, preferred_element_type=F32))
    bidx = lax.broadcasted_iota(jnp.int32, (nb, rows), 0)
    past = bidx < i
    bidx = bidx.astype(F32)
    gate = jnp.where(past, gate, NEG)
    bias = jnp.full((nb, rows), NEG, F32)
    for _ in range(min(MOBA_TOPK, nb)):
        top = jnp.max(gate, axis=0, keepdims=True)
        first = jnp.min(jnp.where(gate == top, bidx, float(nb)), axis=0, keepdims=True)
        pick = bidx == first
        bias = jnp.where(pick & past, 0.0, bias)
        gate = jnp.where(pick, TAKEN, gate)
    qa_ref[dh:dh + nb, :] = bias.astype(BF16)
    if nb < LANES:
        qa_ref[dh + nb:dh + LANES, :] = jnp.zeros((LANES - nb, rows), BF16)

    own = pl.multiple_of(i * blk, blk)
    s = jnp.dot(k_ref[pl.ds(own, blk), 0:dh], q_t, preferred_element_type=F32)
    kpos = lax.broadcasted_iota(jnp.int32, (blk, rows), 0)
    qpos = lax.broadcasted_iota(jnp.int32, (blk, rows), 1) & (blk - 1)
    s = jnp.where(kpos <= qpos, s, NEG)
    m = jnp.max(s, axis=0, keepdims=True)
    m_ref[...] = m
    acc_ref[...] = jnp.dot(vt_ref[i], jnp.exp2(s - m).astype(BF16), preferred_element_type=F32)

    def scores(j, s_ref):
        off = pl.multiple_of(jnp.minimum(j, nb - 1) * blk, blk)
        s_ref[...] = jnp.dot(k_ref[pl.ds(off, blk), :], qa_ref[...], preferred_element_type=F32)

    reread = pl.multiple_of(jnp.minimum(i, 0) * blk, blk)

    def update(s_ref, j):
        v_t = vt_ref[jnp.minimum(j, nb - 1)]
        for c in range(rows // MXU_COLS):
            cols = slice(c * MXU_COLS, (c + 1) * MXU_COLS)
            s = s_ref[pl.ds(reread, blk), cols]
            m_old = m_ref[:, cols]
            m_new = jnp.maximum(m_old, jnp.max(s, axis=0, keepdims=True))
            m_ref[:, cols] = m_new
            p = jnp.exp2(s - m_new).astype(BF16)
            acc_ref[:, cols] = jnp.exp2(m_old - m_new) * acc_ref[:, cols] + jnp.dot(
                v_t, p, preferred_element_type=F32)

    scores(0, sa_ref)

    def quad(t, carry):
        scores(4 * t + 1, sb_ref)
        update(sa_ref, 4 * t)
        scores(4 * t + 2, sa_ref)
        update(sb_ref, 4 * t + 1)
        scores(4 * t + 3, sb_ref)
        update(sa_ref, 4 * t + 2)
        scores(4 * t + 4, sa_ref)
        update(sb_ref, 4 * t + 3)
        return carry
    lax.fori_loop(0, (i + 3) // 4, quad, 0)

    out_t = acc_ref[0:dh, :] / acc_ref[dh:dh + 1, :]
    for g in range(group):
        o_ref[:, g * dh:(g + 1) * dh] = out_t[:, g * blk:(g + 1) * blk].T.astype(o_ref.dtype)


def _moba_attention(q, k, v):
    h, s, dh = q.shape
    hkv = k.shape[0]
    group = h // hkv
    blk = MOBA_BLOCK
    nb = s // blk
    rows = group * blk
    assert nb <= LANES and dh == LANES
    block_of_key = jnp.arange(s, dtype=jnp.int32)[:, None] // blk
    onehot = (block_of_key == jnp.arange(LANES, dtype=jnp.int32)[None, :]).astype(BF16)
    k_aug = jnp.concatenate([k, jnp.broadcast_to(onehot, (hkv, s, LANES))], axis=-1)
    v_t = v.reshape(hkv, nb, blk, dh).transpose(0, 1, 3, 2)
    ones_rows = jnp.zeros((hkv, nb, BF16_SUBLANES, blk), BF16).at[:, :, 0, :].set(1.0)
    vt_aug = jnp.concatenate([v_t, ones_rows], axis=2)
    vrows = dh + BF16_SUBLANES
    return pl.pallas_call(
        _moba_kernel,
        grid=(hkv, nb),
        in_specs=[
            pl.BlockSpec((group, blk, dh), lambda hh, i: (hh, i, 0)),
            pl.BlockSpec((None, s, 2 * LANES), lambda hh, i: (hh, 0, 0)),
            pl.BlockSpec((None, nb, vrows, blk), lambda hh, i: (hh, 0, 0, 0)),
        ],
        out_specs=pl.BlockSpec((blk, group * dh), lambda hh, i: (i, hh)),
        out_shape=jax.ShapeDtypeStruct((s, h * dh), BF16),
        scratch_shapes=[
            pltpu.VMEM((nb, dh), F32),
            pltpu.VMEM((2 * LANES, rows), BF16),
            pltpu.VMEM((blk, rows), F32),
            pltpu.VMEM((blk, rows), F32),
            pltpu.VMEM((1, rows), F32),
            pltpu.VMEM((vrows, rows), F32),
        ],
        compiler_params=_params(
            ("arbitrary", "arbitrary"),
            rows * dh * 2 + s * 2 * LANES * 2 + nb * vrows * blk * 2 + blk * group * dh * 2,
            (2 * blk + vrows + LANES + 1) * rows * 4 + 6 * blk * rows * 4),
        name="moba_attention",
    )(q, k_aug, vt_aug)


def _swa_kernel(q_ref, kp_ref, ko_ref, vp_ref, vo_ref, sink_ref, bias_ref, o_ref, *, nq):
    pairs, span, _ = q_ref.shape
    blk = span // nq
    rows = pairs * blk
    odd = (pl.program_id(0) % 2) == 1
    first_step = pl.program_id(1) == 0

    lane = lax.broadcasted_iota(jnp.int32, (span + blk, LANES), 1)
    lo = lane < B_HEAD_DIM

    def strip(prev_ref, own_ref):
        x = jnp.concatenate([prev_ref[...], own_ref[...]], axis=0)
        swapped = pltpu.roll(x, B_HEAD_DIM, axis=1)
        mine_lo = jnp.where(odd, swapped, x)
        mine_hi = jnp.where(odd, x, swapped)
        return mine_lo, mine_hi

    k_lo, k_hi = strip(kp_ref, ko_ref)
    v_lo, v_hi = strip(vp_ref, vo_ref)
    zero = jnp.zeros_like(k_lo)
    k_both = jnp.where(lo, k_lo, k_hi)
    v_first = jnp.where(lo, v_lo, zero)
    v_second = jnp.where(lo, zero, v_hi)

    qlane = lax.broadcasted_iota(jnp.int32, (rows, LANES), 1)
    q_is_first = qlane < B_HEAD_DIM

    for u in range(nq):
        band = slice(u * blk, (u + 2) * blk)
        q = q_ref[:, u * blk:(u + 1) * blk, :].reshape(rows, LANES)
        qzero = jnp.zeros_like(q)
        bias = bias_ref[1] if u else bias_ref[jnp.where(first_step, 0, 1)]

        def half_head(q_half, v_half, sink):
            s = lax.dot_general(k_both[band], q_half, _NT, preferred_element_type=F32)
            s = s + bias
            m = jnp.maximum(jnp.max(s, axis=0, keepdims=True), sink)
            p = jnp.exp2(s - m)
            denom = jnp.sum(p, axis=0, keepdims=True) + jnp.exp2(sink - m)
            o_t = lax.dot_general(v_half[band], p.astype(BF16), _TN, preferred_element_type=F32)
            return o_t / denom

        out_t = (half_head(jnp.where(q_is_first, q, qzero), v_first, sink_ref[0:1, :])
                 + half_head(jnp.where(q_is_first, qzero, q), v_second, sink_ref[1:2, :]))
        for pr in range(pairs):
            o_ref[u * blk:(u + 1) * blk, pr * LANES:(pr + 1) * LANES] = (
                out_t[:, pr * blk:(pr + 1) * blk].T.astype(o_ref.dtype))


def _swa_attention(q, k, v, sinks):
    npair, s, _ = q.shape
    hkv = k.shape[1] // B_HEAD_DIM
    pairs = npair // hkv
    blk = SWA_BLOCK
    nq = SWA_STEP_BLOCKS
    span = nq * blk
    rows = pairs * blk
    sink_rows = jnp.repeat(
        (sinks.astype(F32) * LOG2_E).reshape(hkv, pairs, 2).transpose(0, 2, 1), blk, axis=-1)
    kk = np.arange(2 * blk)[:, None]
    tt = np.arange(rows)[None, :] % blk
    in_window = (kk > tt) & (kk <= tt + blk)
    band_bias = jnp.asarray(
        np.where(np.stack([in_window & (kk >= blk), in_window]), 0.0, NEG), dtype=F32)
    prev = pl.BlockSpec((blk, LANES), lambda hh, n: (jnp.maximum(n * nq - 1, 0), hh // 2))
    own = pl.BlockSpec((span, LANES), lambda hh, n: (n, hh // 2))
    return pl.pallas_call(
        functools.partial(_swa_kernel, nq=nq),
        grid=(hkv, s // span),
        in_specs=[
            pl.BlockSpec((pairs, span, LANES), lambda hh, n: (hh, n, 0)),
            prev, own, prev, own,
            pl.BlockSpec((None, 2, rows), lambda hh, n: (hh, 0, 0)),
            pl.BlockSpec((2, 2 * blk, rows), lambda hh, n: (0, 0, 0)),
        ],
        out_specs=pl.BlockSpec((span, pairs * LANES), lambda hh, n: (n, hh)),
        out_shape=jax.ShapeDtypeStruct((s, npair * LANES), BF16),
        compiler_params=_params(
            ("parallel", "parallel"),
            2 * pairs * span * LANES * 2 + 2 * (span + blk) * LANES * 2 + 2 * rows * 4
            + 4 * blk * rows * 4,
            12 * blk * rows * 4),
        name="swa_attention",
    )(q, k, k, v, v, sink_rows, band_bias)


def _ffn(x, gain, w_in, w_out, lead):
    (h,) = _rmsnorm(x, gain[None, :], BF16)
    act = _swiglu_up(h, w_in, lead)
    return _matmul_residual(act, w_out, lead, x, 0.5, tm=512, tn=512)


def kernel(x, ffn_norm, ffn_w_in, ffn_w_out, attn_norm, a_w_qkv, a_w_o, kv_norm,
           w_kv_shared, b_w_q, b_sinks, b_w_o, final_norm):
    b, s, d = x.shape
    depth = ffn_norm.shape[0]
    n_a = a_w_qkv.shape[0]
    a_q = a_w_o.shape[1]
    a_kv = (a_w_qkv.shape[2] - a_q) // 2
    b_kv = w_kv_shared.shape[1] // 2
    rope_a = _rope_tables(s, A_HEAD_DIM)
    rope_b = _rope_tables(s, B_HEAD_DIM)
    w_in, w_out = ffn_w_in, ffn_w_out.astype(BF16)
    w_qkv, w_ao, w_kv, w_bq, w_bo = a_w_qkv, a_w_o, w_kv_shared, b_w_q, b_w_o

    outs = []
    for bi in range(b):
        xs = x.reshape(s, d) if b == 1 else x[bi]
        k_sh = v_sh = None
        for layer in range(depth):
            if layer == n_a:
                (h_kv,) = _rmsnorm(xs, kv_norm[None, :], BF16)
                k_sh = _proj(h_kv, w_kv, 0, b_kv, rope=rope_b, half=B_HEAD_DIM // 2)
                v_sh = _proj(h_kv, w_kv, b_kv, b_kv)
            xs = _ffn(xs, ffn_norm[layer, 0], w_in, w_out, (layer, 0))
            (h,) = _rmsnorm(xs, attn_norm[layer][None, :], BF16)
            if layer < n_a:
                w = w_qkv[layer]
                q = _proj(h, w, 0, a_q, rope=rope_a, half=A_HEAD_DIM // 2,
                          scale=A_HEAD_DIM ** -0.5 * LOG2_E, group_major=True)
                k = _proj(h, w, a_q, a_kv, rope=rope_a, half=A_HEAD_DIM // 2, group_major=True)
                v = _proj(h, w, a_q + a_kv, a_kv, group_major=True)
                o = _moba_attention(q, k, v)
                xs = _matmul_residual(o, w_ao, (layer,), xs, 1.0, tm=1024, tn=512)
            else:
                li = layer - n_a
                q = _proj(h, w_bq[li], 0, w_bq.shape[2], rope=rope_b,
                          half=B_HEAD_DIM // 2, scale=B_HEAD_DIM ** -0.5 * LOG2_E, group_major=True)
                o = _swa_attention(q, k_sh, v_sh, b_sinks[li])
                xs = _matmul_residual(o, w_bo, (li,), xs, 1.0, tm=1024, tn=512)
            xs = _ffn(xs, ffn_norm[layer, 1], w_in, w_out, (layer, 1))
        (y,) = _rmsnorm(xs, final_norm[None, :], F32)
        outs.append(y)
    return outs[0].reshape(b, s, d) if b == 1 else jnp.stack(outs, axis=0)
```

```python
import functools

import jax
import jax.numpy as jnp
import numpy as np
from jax import lax
from jax.experimental import pallas as pl
from jax.experimental.pallas import tpu as pltpu

F32 = jnp.float32
BF16 = jnp.bfloat16

RMS_EPS = 1e-6
ROPE_THETA = 10000.0
NEG = -1e30
LOG2_E = 1.4426950408889634
TAKEN = -3e38

A_HEAD_DIM = 128
A_GROUP = 4
MOBA_BLOCK = 256
MOBA_TOPK = 3

B_HEAD_DIM = 64
B_GROUP = 8
SWA_BLOCK = 128
SWA_STEP_BLOCKS = 4

LANES = 128
MXU_COLS = 256
BF16_SUBLANES = 16
UP_TILE_N = 256
V7X_VMEM_CAP = 60000 * 1024

_NT = (((1,), (1,)), ((), ()))
_TN = (((0,), (0,)), ((), ()))


def _params(semantics, block_bytes, scratch_bytes=0):
    need = 2 * block_bytes + scratch_bytes
    return pltpu.CompilerParams(
        dimension_semantics=semantics,
        vmem_limit_bytes=int(min(V7X_VMEM_CAP, max(need, 16 * 1024 * 1024))),
    )


def _lane_tile_sum(x):
    total = x[:, 0:LANES]
    for t in range(1, x.shape[1] // LANES):
        total = total + x[:, t * LANES:(t + 1) * LANES]
    return total


def _row_factor(sumsq_ref, width):
    mean = jnp.sum(sumsq_ref[...], axis=-1, keepdims=True) * (1.0 / width)
    return jnp.broadcast_to(lax.rsqrt(mean + RMS_EPS), sumsq_ref.shape)


def _prenorm_kernel(x_ref, g_ref, sumsq_ref, *o_refs):
    x = x_ref[...]
    sumsq_ref[...] = _lane_tile_sum(x * x)
    for n, o_ref in enumerate(o_refs):
        o_ref[...] = (x * g_ref[n:n + 1, :]).astype(o_ref.dtype)


def _prenorm(x, gains):
    s, d = x.shape
    n = gains.shape[0]
    tm = min(512, s)
    row = pl.BlockSpec((tm, d), lambda i: (i, 0))
    outs = pl.pallas_call(
        _prenorm_kernel,
        grid=(s // tm,),
        in_specs=[row, pl.BlockSpec((n, d), lambda i: (0, 0))],
        out_specs=[pl.BlockSpec((tm, LANES), lambda i: (i, 0))] + [row] * n,
        out_shape=[jax.ShapeDtypeStruct((s, LANES), F32)] + [jax.ShapeDtypeStruct((s, d), BF16)] * n,
        compiler_params=_params(("parallel",), tm * d * (4 + 2 * n), tm * d * 4),
        name="prenorm",
    )(x, gains)
    return outs[0], outs[1:]


def _final_norm_kernel(x_ref, sumsq_ref, g_ref, o_ref):
    mean = jnp.sum(sumsq_ref[...], axis=-1, keepdims=True) * (1.0 / x_ref.shape[1])
    o_ref[...] = x_ref[...] * lax.rsqrt(mean + RMS_EPS) * g_ref[...]


def _final_norm(x, sumsq, gain):
    s, d = x.shape
    tm = min(512, s)
    row = pl.BlockSpec((tm, d), lambda i: (i, 0))
    return pl.pallas_call(
        _final_norm_kernel,
        grid=(s // tm,),
        in_specs=[row, pl.BlockSpec((tm, LANES), lambda i: (i, 0)),
                  pl.BlockSpec((1, d), lambda i: (0, 0))],
        out_specs=row,
        out_shape=jax.ShapeDtypeStruct((s, d), F32),
        compiler_params=_params(("parallel",), tm * d * 8, tm * d * 4),
        name="final_norm",
    )(x, sumsq, gain[None, :])


def _swiglu_up_kernel(h_ref, sumsq_ref, wg_ref, wu_ref, o_ref, r_ref):
    @pl.when(pl.program_id(1) == 0)
    def _row_factors():
        r_ref[...] = _row_factor(sumsq_ref, h_ref.shape[1])

    h = h_ref[...]
    r = r_ref[...]
    g = jnp.dot(h, wg_ref[...].astype(BF16), preferred_element_type=F32)
    u = jnp.dot(h, wu_ref[...].astype(BF16), preferred_element_type=F32)
    for t in range(g.shape[1] // LANES):
        lanes = slice(t * LANES, (t + 1) * LANES)
        gt = g[:, lanes] * r
        o_ref[:, lanes] = (gt * (1.0 / (1.0 + jnp.exp(-gt))) * (u[:, lanes] * r)).astype(o_ref.dtype)


def _swiglu_up(h, sumsq, w_in, lead):
    s, d = h.shape
    f = w_in.shape[-1] // 2
    tm = min(1024, s)
    tn = UP_TILE_N
    nj = f // tn
    squeezed = (None,) * len(lead)
    return pl.pallas_call(
        _swiglu_up_kernel,
        grid=(s // tm, nj),
        in_specs=[
            pl.BlockSpec((tm, d), lambda i, j: (i, 0)),
            pl.BlockSpec((tm, LANES), lambda i, j: (i, 0)),
            pl.BlockSpec(squeezed + (d, tn), lambda i, j: lead + (0, j)),
            pl.BlockSpec(squeezed + (d, tn), lambda i, j: lead + (0, j + nj)),
        ],
        out_specs=pl.BlockSpec((tm, tn), lambda i, j: (i, j)),
        out_shape=jax.ShapeDtypeStruct((s, f), BF16),
        scratch_shapes=[pltpu.VMEM((tm, LANES), F32)],
        compiler_params=_params(
            ("parallel", "arbitrary"),
            tm * d * 2 + 2 * d * tn * w_in.dtype.itemsize + tm * tn * 2,
            4 * tm * tn * 4 + 2 * d * tn * 2),
        name="swiglu_up",
    )(h, sumsq, w_in, w_in)


def _matmul_residual_kernel(a_ref, w_ref, r_ref, *refs, alpha, n_gains):
    if n_gains:
        g_ref, o_ref, sumsq_ref, *h_refs = refs
    else:
        o_ref, sumsq_ref = refs
        h_refs = []
    acc = jnp.dot(a_ref[...], w_ref[...].astype(BF16), preferred_element_type=F32)
    x_new = r_ref[...] + alpha * acc
    o_ref[...] = x_new
    part = _lane_tile_sum(x_new * x_new)

    @pl.when(pl.program_id(1) == 0)
    def _first():
        sumsq_ref[...] = part

    @pl.when(pl.program_id(1) != 0)
    def _rest():
        sumsq_ref[...] += part

    for n, h_ref in enumerate(h_refs):
        h_ref[...] = (x_new * g_ref[n:n + 1, :]).astype(h_ref.dtype)


def _matmul_residual(a, w, lead, res, alpha, tm, tn, next_gains):
    s, k = a.shape
    n = w.shape[-1]
    ng = len(next_gains)
    tm, tn = min(tm, s), min(tn, n)
    squeezed = (None,) * len(lead)
    tile = pl.BlockSpec((tm, tn), lambda i, j: (i, j))
    in_specs = [
        pl.BlockSpec((tm, k), lambda i, j: (i, 0)),
        pl.BlockSpec(squeezed + (k, tn), lambda i, j: lead + (0, j)),
        tile,
    ]
    args = [a, w, res]
    if ng:
        in_specs.append(pl.BlockSpec((ng, tn), lambda i, j: (0, j)))
        args.append(jnp.stack(next_gains))
    outs = pl.pallas_call(
        functools.partial(_matmul_residual_kernel, alpha=alpha, n_gains=ng),
        grid=(s // tm, n // tn),
        in_specs=in_specs,
        out_specs=[tile, pl.BlockSpec((tm, LANES), lambda i, j: (i, 0))] + [tile] * ng,
        out_shape=[jax.ShapeDtypeStruct((s, n), F32), jax.ShapeDtypeStruct((s, LANES), F32)]
        + [jax.ShapeDtypeStruct((s, n), BF16)] * ng,
        compiler_params=_params(
            ("parallel", "arbitrary"),
            tm * k * 2 + k * tn * w.dtype.itemsize + 2 * tm * tn * 4 + tm * LANES * 4
            + ng * tm * tn * 2,
            3 * tm * tn * 4 + k * tn * 2),
        name="matmul_residual",
    )(*args)
    return outs[0], outs[1], outs[2:]


def _rope_tables(seq, dim):
    half = dim // 2
    inv = 1.0 / (ROPE_THETA ** (jnp.arange(0, dim, 2, dtype=F32) / dim))
    ang = jnp.arange(seq, dtype=F32)[:, None] * inv[None, :]
    cos, sin = jnp.cos(ang), jnp.sin(ang)
    zero = jnp.zeros_like(sin)
    reps = LANES // dim
    cos_t = jnp.tile(jnp.concatenate([cos, cos], axis=-1), (1, reps))
    s_lo = jnp.tile(jnp.concatenate([-sin, zero], axis=-1), (1, reps))
    s_hi = jnp.tile(jnp.concatenate([zero, sin], axis=-1), (1, reps))
    return cos_t, s_lo, s_hi


def _proj_kernel(h_ref, sumsq_ref, w_ref, *refs, half, scale, group_major):
    acc = jnp.dot(h_ref[...], w_ref[...].astype(BF16), preferred_element_type=F32)
    if half:
        cos_ref, slo_ref, shi_ref, o_ref = refs
        cos, s_lo, s_hi = cos_ref[...], slo_ref[...], shi_ref[...]
    else:
        (o_ref,) = refs
    r = _row_factor(sumsq_ref, h_ref.shape[1]) * scale
    for g in range(acc.shape[1] // LANES):
        x = acc[:, g * LANES:(g + 1) * LANES] * r
        if half:
            x = (x * cos + pltpu.roll(x, LANES - half, axis=1) * s_lo
                 + pltpu.roll(x, half, axis=1) * s_hi)
        if group_major:
            o_ref[g] = x.astype(o_ref.dtype)
        else:
            o_ref[:, g * LANES:(g + 1) * LANES] = x.astype(o_ref.dtype)


def _proj(h, sumsq, w, col0, ncols, *, rope=None, half=0, scale=1.0, group_major=False):
    s, k = h.shape
    tm = min(1024, s)
    tn = min(512, ncols)
    j0 = col0 // tn
    in_specs = [
        pl.BlockSpec((tm, k), lambda i, j: (i, 0)),
        pl.BlockSpec((tm, LANES), lambda i, j: (i, 0)),
        pl.BlockSpec((k, tn), lambda i, j: (0, j + j0)),
    ]
    args = [h, sumsq, w]
    if half:
        tab = pl.BlockSpec((tm, LANES), lambda i, j: (i, 0))
        in_specs += [tab, tab, tab]
        args += list(rope)
    if group_major:
        gpt = tn // LANES
        out_spec = pl.BlockSpec((gpt, tm, LANES), lambda i, j: (j, i, 0))
        out_shape = jax.ShapeDtypeStruct((ncols // LANES, s, LANES), BF16)
    else:
        out_spec = pl.BlockSpec((tm, tn), lambda i, j: (i, j))
        out_shape = jax.ShapeDtypeStruct((s, ncols), BF16)
    return pl.pallas_call(
        functools.partial(_proj_kernel, half=half, scale=scale, group_major=group_major),
        grid=(s // tm, ncols // tn),
        in_specs=in_specs,
        out_specs=out_spec,
        out_shape=out_shape,
        compiler_params=_params(
            ("parallel", "arbitrary"),
            tm * k * 2 + k * tn * w.dtype.itemsize + 4 * tm * LANES * 4 + tm * tn * 2,
            3 * tm * tn * 4 + k * tn * 2),
        name="proj",
    )(*args)


def _moba_kernel(q_ref, k_ref, vt_ref, o_ref, kmean_ref, qa_ref, sa_ref, sb_ref, m_ref, acc_ref):
    group, blk, dh = q_ref.shape
    nb = vt_ref.shape[0]
    rows = group * blk
    i = pl.program_id(1)

    @pl.when(i == 0)
    def _block_means():
        def body(j, carry):
            kj = k_ref[pl.ds(pl.multiple_of(j * blk, blk), blk), 0:dh].astype(F32)
            kmean_ref[pl.ds(j, 1), :] = jnp.sum(kj, axis=0, keepdims=True) * (1.0 / blk)
            return carry
        lax.fori_loop(0, nb, body, 0)

    q_t = q_ref[...].reshape(rows, dh).astype(F32).T.astype(BF16)
    qa_ref[0:dh, :] = q_t

    km = kmean_ref[...]
    km_hi = km.astype(BF16)
    km_lo = (km - km_hi.astype(F32)).astype(BF16)
    gate = (jnp.dot(km_hi, q_t, preferred_element_type=F32)
            + jnp.dot(km_lo, q_t, preferred_element_type=F32))
    bidx = lax.broadcasted_iota(jnp.int32, (nb, rows), 0)
    past = bidx < i
    bidx = bidx.astype(F32)
    gate = jnp.where(past, gate, NEG)
    bias = jnp.full((nb, rows), NEG, F32)
    for _ in range(min(MOBA_TOPK, nb)):
        top = jnp.max(gate, axis=0, keepdims=True)
        first = jnp.min(jnp.where(gate == top, bidx, float(nb)), axis=0, keepdims=True)
        pick = bidx == first
        bias = jnp.where(pick & past, 0.0, bias)
        gate = jnp.where(pick, TAKEN, gate)
    qa_ref[dh:dh + nb, :] = bias.astype(BF16)
    if nb < LANES:
        qa_ref[dh + nb:dh + LANES, :] = jnp.zeros((LANES - nb, rows), BF16)

    own = pl.multiple_of(i * blk, blk)
    s = jnp.dot(k_ref[pl.ds(own, blk), 0:dh], q_t, preferred_element_type=F32)
    kpos = lax.broadcasted_iota(jnp.int32, (blk, rows), 0)
    qpos = lax.broadcasted_iota(jnp.int32, (blk, rows), 1) & (blk - 1)
    s = jnp.where(kpos <= qpos, s, NEG)
    m = jnp.max(s, axis=0, keepdims=True)
    m_ref[...] = m
    acc_ref[...] = jnp.dot(vt_ref[i], jnp.exp2(s - m).astype(BF16), preferred_element_type=F32)

    def scores(j, s_ref):
        off = pl.multiple_of(jnp.minimum(j, nb - 1) * blk, blk)
        s_ref[...] = jnp.dot(k_ref[pl.ds(off, blk), :], qa_ref[...], preferred_element_type=F32)

    reread = pl.multiple_of(jnp.minimum(i, 0) * blk, blk)

    def update(s_ref, j):
        v_t = vt_ref[jnp.minimum(j, nb - 1)]
        for c in range(rows // MXU_COLS):
            cols = slice(c * MXU_COLS, (c + 1) * MXU_COLS)
            s = s_ref[pl.ds(reread, blk), cols]
            m_old = m_ref[:, cols]
            m_new = jnp.maximum(m_old, jnp.max(s, axis=0, keepdims=True))
            m_ref[:, cols] = m_new
            p = jnp.exp2(s - m_new).astype(BF16)
            acc_ref[:, cols] = jnp.exp2(m_old - m_new) * acc_ref[:, cols] + jnp.dot(
                v_t, p, preferred_element_type=F32)

    scores(0, sa_ref)

    def quad(t, carry):
        scores(4 * t + 1, sb_ref)
        update(sa_ref, 4 * t)
        scores(4 * t + 2, sa_ref)
        update(sb_ref, 4 * t + 1)
        scores(4 * t + 3, sb_ref)
        update(sa_ref, 4 * t + 2)
        scores(4 * t + 4, sa_ref)
        update(sb_ref, 4 * t + 3)
        return carry
    lax.fori_loop(0, (i + 3) // 4, quad, 0)

    out_t = acc_ref[0:dh, :] / acc_ref[dh:dh + 1, :]
    for g in range(group):
        o_ref[:, g * dh:(g + 1) * dh] = out_t[:, g * blk:(g + 1) * blk].T.astype(o_ref.dtype)


def _moba_attention(q, k, v):
    h, s, dh = q.shape
    hkv = k.shape[0]
    group = h // hkv
    blk = MOBA_BLOCK
    nb = s // blk
    rows = group * blk
    assert nb <= LANES and dh == LANES
    block_of_key = jnp.arange(s, dtype=jnp.int32)[:, None] // blk
    onehot = (block_of_key == jnp.arange(LANES, dtype=jnp.int32)[None, :]).astype(BF16)
    k_aug = jnp.concatenate([k, jnp.broadcast_to(onehot, (hkv, s, LANES))], axis=-1)
    v_t = v.reshape(hkv, nb, blk, dh).transpose(0, 1, 3, 2)
    ones_rows = jnp.zeros((hkv, nb, BF16_SUBLANES, blk), BF16).at[:, :, 0, :].set(1.0)
    vt_aug = jnp.concatenate([v_t, ones_rows], axis=2)
    vrows = dh + BF16_SUBLANES
    return pl.pallas_call(
        _moba_kernel,
        grid=(hkv, nb),
        in_specs=[
            pl.BlockSpec((group, blk, dh), lambda hh, i: (hh, i, 0)),
            pl.BlockSpec((None, s, 2 * LANES), lambda hh, i: (hh, 0, 0)),
            pl.BlockSpec((None, nb, vrows, blk), lambda hh, i: (hh, 0, 0, 0)),
        ],
        out_specs=pl.BlockSpec((blk, group * dh), lambda hh, i: (i, hh)),
        out_shape=jax.ShapeDtypeStruct((s, h * dh), BF16),
        scratch_shapes=[
            pltpu.VMEM((nb, dh), F32),
            pltpu.VMEM((2 * LANES, rows), BF16),
            pltpu.VMEM((blk, rows), F32),
            pltpu.VMEM((blk, rows), F32),
            pltpu.VMEM((1, rows), F32),
            pltpu.VMEM((vrows, rows), F32),
        ],
        compiler_params=_params(
            ("arbitrary", "arbitrary"),
            rows * dh * 2 + s * 2 * LANES * 2 + nb * vrows * blk * 2 + blk * group * dh * 2,
            (2 * blk + vrows + LANES + 1) * rows * 4 + 6 * blk * rows * 4),
        name="moba_attention",
    )(q, k_aug, vt_aug)


def _swa_kernel(q_ref, kp_ref, ko_ref, vp_ref, vo_ref, sink_ref, bias_ref, o_ref, *, nq):
    pairs, span, _ = q_ref.shape
    blk = span // nq
    rows = pairs * blk
    odd = (pl.program_id(0) % 2) == 1
    first_step = pl.program_id(1) == 0

    lane = lax.broadcasted_iota(jnp.int32, (span + blk, LANES), 1)
    lo = lane < B_HEAD_DIM

    def strip(prev_ref, own_ref):
        x = jnp.concatenate([prev_ref[...], own_ref[...]], axis=0)
        swapped = pltpu.roll(x, B_HEAD_DIM, axis=1)
        mine_lo = jnp.where(odd, swapped, x)
        mine_hi = jnp.where(odd, x, swapped)
        return mine_lo, mine_hi

    k_lo, k_hi = strip(kp_ref, ko_ref)
    v_lo, v_hi = strip(vp_ref, vo_ref)
    zero = jnp.zeros_like(k_lo)
    k_both = jnp.where(lo, k_lo, k_hi)
    v_first = jnp.where(lo, v_lo, zero)
    v_second = jnp.where(lo, zero, v_hi)

    qlane = lax.broadcasted_iota(jnp.int32, (rows, LANES), 1)
    q_is_first = qlane < B_HEAD_DIM

    for u in range(nq):
        band = slice(u * blk, (u + 2) * blk)
        q = q_ref[:, u * blk:(u + 1) * blk, :].reshape(rows, LANES)
        qzero = jnp.zeros_like(q)
        bias = bias_ref[1] if u else bias_ref[jnp.where(first_step, 0, 1)]

        def half_head(q_half, v_half, sink):
            s = lax.dot_general(k_both[band], q_half, _NT, preferred_element_type=F32)
            s = s + bias
            m = jnp.maximum(jnp.max(s, axis=0, keepdims=True), sink)
            p = jnp.exp2(s - m)
            denom = jnp.sum(p, axis=0, keepdims=True) + jnp.exp2(sink - m)
            o_t = lax.dot_general(v_half[band], p.astype(BF16), _TN, preferred_element_type=F32)
            return o_t / denom

        out_t = (half_head(jnp.where(q_is_first, q, qzero), v_first, sink_ref[0:1, :])
                 + half_head(jnp.where(q_is_first, qzero, q), v_second, sink_ref[1:2, :]))
        for pr in range(pairs):
            o_ref[u * blk:(u + 1) * blk, pr * LANES:(pr + 1) * LANES] = (
                out_t[:, pr * blk:(pr + 1) * blk].T.astype(o_ref.dtype))


def _swa_attention(q, k, v, sinks):
    npair, s, _ = q.shape
    hkv = k.shape[1] // B_HEAD_DIM
    pairs = npair // hkv
    blk = SWA_BLOCK
    nq = SWA_STEP_BLOCKS
    span = nq * blk
    rows = pairs * blk
    sink_rows = jnp.repeat(
        (sinks.astype(F32) * LOG2_E).reshape(hkv, pairs, 2).transpose(0, 2, 1), blk, axis=-1)
    kk = np.arange(2 * blk)[:, None]
    tt = np.arange(rows)[None, :] % blk
    in_window = (kk > tt) & (kk <= tt + blk)
    band_bias = jnp.asarray(
        np.where(np.stack([in_window & (kk >= blk), in_window]), 0.0, NEG), dtype=F32)
    prev = pl.BlockSpec((blk, LANES), lambda hh, n: (jnp.maximum(n * nq - 1, 0), hh // 2))
    own = pl.BlockSpec((span, LANES), lambda hh, n: (n, hh // 2))
    return pl.pallas_call(
        functools.partial(_swa_kernel, nq=nq),
        grid=(hkv, s // span),
        in_specs=[
            pl.BlockSpec((pairs, span, LANES), lambda hh, n: (hh, n, 0)),
            prev, own, prev, own,
            pl.BlockSpec((None, 2, rows), lambda hh, n: (hh, 0, 0)),
            pl.BlockSpec((2, 2 * blk, rows), lambda hh, n: (0, 0, 0)),
        ],
        out_specs=pl.BlockSpec((span, pairs * LANES), lambda hh, n: (n, hh)),
        out_shape=jax.ShapeDtypeStruct((s, npair * LANES), BF16),
        compiler_params=_params(
            ("parallel", "parallel"),
            2 * pairs * span * LANES * 2 + 2 * (span + blk) * LANES * 2 + 2 * rows * 4
            + 4 * blk * rows * 4,
            12 * blk * rows * 4),
        name="swa_attention",
    )(q, k, k, v, v, sink_rows, band_bias)


def kernel(x, ffn_norm, ffn_w_in, ffn_w_out, attn_norm, a_w_qkv, a_w_o, kv_norm,
           w_kv_shared, b_w_q, b_sinks, b_w_o, final_norm):
    b, s, d = x.shape
    depth = ffn_norm.shape[0]
    n_a = a_w_qkv.shape[0]
    a_q = a_w_o.shape[1]
    a_kv = (a_w_qkv.shape[2] - a_q) // 2
    b_kv = w_kv_shared.shape[1] // 2
    rope_a = _rope_tables(s, A_HEAD_DIM)
    rope_b = _rope_tables(s, B_HEAD_DIM)
    w_in, w_out = ffn_w_in, ffn_w_out.astype(BF16)
    w_qkv, w_ao = a_w_qkv.astype(BF16), a_w_o.astype(BF16)
    w_kv, w_bq, w_bo = w_kv_shared.astype(BF16), b_w_q.astype(BF16), b_w_o.astype(BF16)

    def gains_entering(layer):
        if layer == depth:
            return []
        return ([kv_norm] if layer == n_a else []) + [ffn_norm[layer, 0]]

    outs = []
    for bi in range(b):
        xs = x.reshape(s, d) if b == 1 else x[bi]
        sumsq, hs = _prenorm(xs, jnp.stack(gains_entering(0)))
        k_sh = v_sh = None
        for layer in range(depth):
            hs = list(hs)
            if layer == n_a:
                h_kv = hs.pop(0)
                k_sh = _proj(h_kv, sumsq, w_kv, 0, b_kv, rope=rope_b, half=B_HEAD_DIM // 2)
                v_sh = _proj(h_kv, sumsq, w_kv, b_kv, b_kv)
            act = _swiglu_up(hs[0], sumsq, w_in, (layer, 0))
            xs, sumsq, (h,) = _matmul_residual(act, w_out, (layer, 0), xs, 0.5, 512, 512,
                                               [attn_norm[layer]])
            if layer < n_a:
                w = w_qkv[layer]
                q = _proj(h, sumsq, w, 0, a_q, rope=rope_a, half=A_HEAD_DIM // 2,
                          scale=A_HEAD_DIM ** -0.5 * LOG2_E, group_major=True)
                k = _proj(h, sumsq, w, a_q, a_kv, rope=rope_a, half=A_HEAD_DIM // 2,
                          group_major=True)
                v = _proj(h, sumsq, w, a_q + a_kv, a_kv, group_major=True)
                o = _moba_attention(q, k, v)
                w_o, lead = w_ao, (layer,)
            else:
                li = layer - n_a
                q = _proj(h, sumsq, w_bq[li], 0, w_bq.shape[2], rope=rope_b,
                          half=B_HEAD_DIM // 2, scale=B_HEAD_DIM ** -0.5 * LOG2_E, group_major=True)
                o = _swa_attention(q, k_sh, v_sh, b_sinks[li])
                w_o, lead = w_bo, (li,)
            xs, sumsq, (h,) = _matmul_residual(o, w_o, lead, xs, 1.0, 1024, 512,
                                               [ffn_norm[layer, 1]])
            act = _swiglu_up(h, sumsq, w_in, (layer, 1))
            xs, sumsq, hs = _matmul_residual(act, w_out, (layer, 1), xs, 0.5, 512, 512,
                                             gains_entering(layer + 1))
        outs.append(_final_norm(xs, sumsq, final_norm))
    return outs[0].reshape(b, s, d) if b == 1 else jnp.stack(outs, axis=0)
```

```python
import functools

import jax
import jax.numpy as jnp
import numpy as np
from jax import lax
from jax.experimental import pallas as pl
from jax.experimental.pallas import tpu as pltpu

F32 = jnp.float32
BF16 = jnp.bfloat16

RMS_EPS = 1e-6
ROPE_THETA = 10000.0
NEG = -1e30
LOG2_E = 1.4426950408889634
TAKEN = -3e38

A_HEAD_DIM = 128
A_GROUP = 4
MOBA_BLOCK = 256
MOBA_TOPK = 3

B_HEAD_DIM = 64
B_GROUP = 8
SWA_BLOCK = 128
SWA_STEP_BLOCKS = 4

LANES = 128
MXU_COLS = 256
BF16_SUBLANES = 16
UP_TILE_N = 256
V7X_VMEM_CAP = 60000 * 1024

_NT = (((1,), (1,)), ((), ()))
_TN = (((0,), (0,)), ((), ()))


def _params(semantics, block_bytes, scratch_bytes=0):
    need = 2 * block_bytes + scratch_bytes
    return pltpu.CompilerParams(
        dimension_semantics=semantics,
        vmem_limit_bytes=int(min(V7X_VMEM_CAP, max(need, 16 * 1024 * 1024))),
    )


def _lane_tile_sum(x):
    total = x[:, 0:LANES]
    for t in range(1, x.shape[1] // LANES):
        total = total + x[:, t * LANES:(t + 1) * LANES]
    return total


def _row_factor(sumsq_ref, width):
    mean = jnp.sum(sumsq_ref[...], axis=-1, keepdims=True) * (1.0 / width)
    return jnp.broadcast_to(lax.rsqrt(mean + RMS_EPS), sumsq_ref.shape)


def _prenorm_kernel(x_ref, g_ref, sumsq_ref, *o_refs):
    x = x_ref[...]
    sumsq_ref[...] = _lane_tile_sum(x * x)
    for n, o_ref in enumerate(o_refs):
        o_ref[...] = (x * g_ref[n:n + 1, :]).astype(o_ref.dtype)


def _prenorm(x, gains):
    s, d = x.shape
    n = gains.shape[0]
    tm = min(512, s)
    row = pl.BlockSpec((tm, d), lambda i: (i, 0))
    outs = pl.pallas_call(
        _prenorm_kernel,
        grid=(s // tm,),
        in_specs=[row, pl.BlockSpec((n, d), lambda i: (0, 0))],
        out_specs=[pl.BlockSpec((tm, LANES), lambda i: (i, 0))] + [row] * n,
        out_shape=[jax.ShapeDtypeStruct((s, LANES), F32)] + [jax.ShapeDtypeStruct((s, d), BF16)] * n,
        compiler_params=_params(("parallel",), tm * d * (4 + 2 * n), tm * d * 4),
        name="prenorm",
    )(x, gains)
    return outs[0], outs[1:]


def _final_norm_kernel(x_ref, sumsq_ref, g_ref, o_ref):
    mean = jnp.sum(sumsq_ref[...], axis=-1, keepdims=True) * (1.0 / x_ref.shape[1])
    o_ref[...] = x_ref[...] * lax.rsqrt(mean + RMS_EPS) * g_ref[...]


def _final_norm(x, sumsq, gain):
    s, d = x.shape
    tm = min(512, s)
    row = pl.BlockSpec((tm, d), lambda i: (i, 0))
    return pl.pallas_call(
        _final_norm_kernel,
        grid=(s // tm,),
        in_specs=[row, pl.BlockSpec((tm, LANES), lambda i: (i, 0)),
                  pl.BlockSpec((1, d), lambda i: (0, 0))],
        out_specs=row,
        out_shape=jax.ShapeDtypeStruct((s, d), F32),
        compiler_params=_params(("parallel",), tm * d * 8, tm * d * 4),
        name="final_norm",
    )(x, sumsq, gain[None, :])


def _swiglu_up_kernel(h_ref, sumsq_ref, wg_ref, wu_ref, o_ref, r_ref):
    @pl.when(pl.program_id(1) == 0)
    def _row_factors():
        r_ref[...] = _row_factor(sumsq_ref, h_ref.shape[1])

    h = h_ref[...]
    r = r_ref[...]
    g = jnp.dot(h, wg_ref[...].astype(BF16), preferred_element_type=F32)
    u = jnp.dot(h, wu_ref[...].astype(BF16), preferred_element_type=F32)
    for t in range(g.shape[1] // LANES):
        lanes = slice(t * LANES, (t + 1) * LANES)
        gt = g[:, lanes] * r
        o_ref[:, lanes] = (gt * (1.0 / (1.0 + jnp.exp(-gt))) * (u[:, lanes] * r)).astype(o_ref.dtype)


def _swiglu_up(h, sumsq, w_in, lead):
    s, d = h.shape
    f = w_in.shape[-1] // 2
    tm = min(1024, s)
    tn = UP_TILE_N
    nj = f // tn
    squeezed = (None,) * len(lead)
    return pl.pallas_call(
        _swiglu_up_kernel,
        grid=(s // tm, nj),
        in_specs=[
            pl.BlockSpec((tm, d), lambda i, j: (i, 0)),
            pl.BlockSpec((tm, LANES), lambda i, j: (i, 0)),
            pl.BlockSpec(squeezed + (d, tn), lambda i, j: lead + (0, j)),
            pl.BlockSpec(squeezed + (d, tn), lambda i, j: lead + (0, j + nj)),
        ],
        out_specs=pl.BlockSpec((tm, tn), lambda i, j: (i, j)),
        out_shape=jax.ShapeDtypeStruct((s, f), BF16),
        scratch_shapes=[pltpu.VMEM((tm, LANES), F32)],
        compiler_params=_params(
            ("parallel", "arbitrary"),
            tm * d * 2 + 2 * d * tn * w_in.dtype.itemsize + tm * tn * 2,
            4 * tm * tn * 4 + 2 * d * tn * 2),
        name="swiglu_up",
    )(h, sumsq, w_in, w_in)


def _matmul_residual_kernel(a_ref, w_ref, r_ref, *refs, alpha, n_gains):
    if n_gains:
        g_ref, o_ref, sumsq_ref, *h_refs = refs
    else:
        o_ref, sumsq_ref = refs
        h_refs = []
    acc = jnp.dot(a_ref[...], w_ref[...].astype(BF16), preferred_element_type=F32)
    x_new = r_ref[...] + alpha * acc
    o_ref[...] = x_new
    part = _lane_tile_sum(x_new * x_new)

    @pl.when(pl.program_id(1) == 0)
    def _first():
        sumsq_ref[...] = part

    @pl.when(pl.program_id(1) != 0)
    def _rest():
        sumsq_ref[...] += part

    for n, h_ref in enumerate(h_refs):
        h_ref[...] = (x_new * g_ref[n:n + 1, :]).astype(h_ref.dtype)


def _matmul_residual(a, w, lead, res, alpha, tm, tn, next_gains):
    s, k = a.shape
    n = w.shape[-1]
    ng = len(next_gains)
    tm, tn = min(tm, s), min(tn, n)
    squeezed = (None,) * len(lead)
    tile = pl.BlockSpec((tm, tn), lambda i, j: (i, j))
    in_specs = [
        pl.BlockSpec((tm, k), lambda i, j: (i, 0)),
        pl.BlockSpec(squeezed + (k, tn), lambda i, j: lead + (0, j)),
        tile,
    ]
    args = [a, w, res]
    if ng:
        in_specs.append(pl.BlockSpec((ng, tn), lambda i, j: (0, j)))
        args.append(jnp.stack(next_gains))
    outs = pl.pallas_call(
        functools.partial(_matmul_residual_kernel, alpha=alpha, n_gains=ng),
        grid=(s // tm, n // tn),
        in_specs=in_specs,
        out_specs=[tile, pl.BlockSpec((tm, LANES), lambda i, j: (i, 0))] + [tile] * ng,
        out_shape=[jax.ShapeDtypeStruct((s, n), F32), jax.ShapeDtypeStruct((s, LANES), F32)]
        + [jax.ShapeDtypeStruct((s, n), BF16)] * ng,
        compiler_params=_params(
            ("parallel", "arbitrary"),
            tm * k * 2 + k * tn * w.dtype.itemsize + 2 * tm * tn * 4 + tm * LANES * 4
            + ng * tm * tn * 2,
            3 * tm * tn * 4 + k * tn * 2),
        name="matmul_residual",
    )(*args)
    return outs[0], outs[1], outs[2:]


def _rope_tables(seq, dim):
    half = dim // 2
    inv = 1.0 / (ROPE_THETA ** (jnp.arange(0, dim, 2, dtype=F32) / dim))
    ang = jnp.arange(seq, dtype=F32)[:, None] * inv[None, :]
    cos, sin = jnp.cos(ang), jnp.sin(ang)
    zero = jnp.zeros_like(sin)
    reps = LANES // dim
    cos_t = jnp.tile(jnp.concatenate([cos, cos], axis=-1), (1, reps))
    s_lo = jnp.tile(jnp.concatenate([-sin, zero], axis=-1), (1, reps))
    s_hi = jnp.tile(jnp.concatenate([zero, sin], axis=-1), (1, reps))
    return cos_t, s_lo, s_hi


def _proj_kernel(h_ref, sumsq_ref, w_ref, *refs, half, scale, group_major):
    acc = jnp.dot(h_ref[...], w_ref[...].astype(BF16), preferred_element_type=F32)
    if half:
        cos_ref, slo_ref, shi_ref, o_ref = refs
        cos, s_lo, s_hi = cos_ref[...], slo_ref[...], shi_ref[...]
    else:
        (o_ref,) = refs
    r = _row_factor(sumsq_ref, h_ref.shape[1]) * scale
    for g in range(acc.shape[1] // LANES):
        x = acc[:, g * LANES:(g + 1) * LANES] * r
        if half:
            x = (x * cos + pltpu.roll(x, LANES - half, axis=1) * s_lo
                 + pltpu.roll(x, half, axis=1) * s_hi)
        if group_major:
            o_ref[g] = x.astype(o_ref.dtype)
        else:
            o_ref[:, g * LANES:(g + 1) * LANES] = x.astype(o_ref.dtype)


def _proj(h, sumsq, w, col0, ncols, *, rope=None, half=0, scale=1.0, group_major=False):
    s, k = h.shape
    tm = min(1024, s)
    tn = min(512, ncols)
    j0 = col0 // tn
    in_specs = [
        pl.BlockSpec((tm, k), lambda i, j: (i, 0)),
        pl.BlockSpec((tm, LANES), lambda i, j: (i, 0)),
        pl.BlockSpec((k, tn), lambda i, j: (0, j + j0)),
    ]
    args = [h, sumsq, w]
    if half:
        tab = pl.BlockSpec((tm, LANES), lambda i, j: (i, 0))
        in_specs += [tab, tab, tab]
        args += list(rope)
    if group_major:
        gpt = tn // LANES
        out_spec = pl.BlockSpec((gpt, tm, LANES), lambda i, j: (j, i, 0))
        out_shape = jax.ShapeDtypeStruct((ncols // LANES, s, LANES), BF16)
    else:
        out_spec = pl.BlockSpec((tm, tn), lambda i, j: (i, j))
        out_shape = jax.ShapeDtypeStruct((s, ncols), BF16)
    return pl.pallas_call(
        functools.partial(_proj_kernel, half=half, scale=scale, group_major=group_major),
        grid=(s // tm, ncols // tn),
        in_specs=in_specs,
        out_specs=out_spec,
        out_shape=out_shape,
        compiler_params=_params(
            ("parallel", "arbitrary"),
            tm * k * 2 + k * tn * w.dtype.itemsize + 4 * tm * LANES * 4 + tm * tn * 2,
            3 * tm * tn * 4 + k * tn * 2),
        name="proj",
    )(*args)


def _moba_kernel(q_ref, k_ref, vt_ref, o_ref, kmean_ref, qa_ref, sa_ref, sb_ref, m_ref, acc_ref):
    group, blk, dh = q_ref.shape
    nb = vt_ref.shape[0]
    rows = group * blk
    n_col = rows // MXU_COLS
    i = pl.program_id(1)

    def col(c):
        return slice(c * MXU_COLS, (c + 1) * MXU_COLS)

    @pl.when(i == 0)
    def _block_means():
        def body(j, carry):
            kj = k_ref[pl.ds(pl.multiple_of(j * blk, blk), blk), 0:dh].astype(F32)
            kmean_ref[pl.ds(j, 1), :] = jnp.sum(kj, axis=0, keepdims=True) * (1.0 / blk)
            return carry
        lax.fori_loop(0, nb, body, 0)

    q_t = q_ref[...].reshape(rows, dh).astype(F32).T.astype(BF16)
    for c in range(n_col):
        qa_ref[c, 0:dh, :] = q_t[:, col(c)]

    km = kmean_ref[...]
    km_hi = km.astype(BF16)
    km_lo = (km - km_hi.astype(F32)).astype(BF16)
    gate = (jnp.dot(km_hi, q_t, preferred_element_type=F32)
            + jnp.dot(km_lo, q_t, preferred_element_type=F32))
    bidx = lax.broadcasted_iota(jnp.int32, (nb, rows), 0)
    past = bidx < i
    bidx = bidx.astype(F32)
    gate = jnp.where(past, gate, NEG)
    bias = jnp.full((nb, rows), NEG, F32)
    for _ in range(min(MOBA_TOPK, nb)):
        top = jnp.max(gate, axis=0, keepdims=True)
        first = jnp.min(jnp.where(gate == top, bidx, float(nb)), axis=0, keepdims=True)
        pick = bidx == first
        bias = jnp.where(pick & past, 0.0, bias)
        gate = jnp.where(pick, TAKEN, gate)
    bias = bias.astype(BF16)
    for c in range(n_col):
        qa_ref[c, dh:dh + nb, :] = bias[:, col(c)]
        if nb < LANES:
            qa_ref[c, dh + nb:dh + LANES, :] = jnp.zeros((LANES - nb, MXU_COLS), BF16)

    own = pl.multiple_of(i * blk, blk)
    s = jnp.dot(k_ref[pl.ds(own, blk), 0:dh], q_t, preferred_element_type=F32)
    kpos = lax.broadcasted_iota(jnp.int32, (blk, rows), 0)
    qpos = lax.broadcasted_iota(jnp.int32, (blk, rows), 1) & (blk - 1)
    s = jnp.where(kpos <= qpos, s, NEG)
    m = jnp.max(s, axis=0, keepdims=True)
    acc = jnp.dot(vt_ref[i], jnp.exp2(s - m).astype(BF16), preferred_element_type=F32)
    for c in range(n_col):
        m_ref[c] = m[:, col(c)]
        acc_ref[c] = acc[:, col(c)]

    def scores(j, s_ref):
        off = pl.multiple_of(jnp.minimum(j, nb - 1) * blk, blk)
        keys = k_ref[pl.ds(off, blk), :]
        for c in range(n_col):
            s_ref[c] = jnp.dot(keys, qa_ref[c], preferred_element_type=F32)

    reread = pl.multiple_of(jnp.minimum(i, 0) * blk, blk)

    def update(s_ref, j):
        v_t = vt_ref[jnp.minimum(j, nb - 1)]
        for c in range(n_col):
            s = s_ref[c, pl.ds(reread, blk), :]
            m_old = m_ref[c]
            m_new = jnp.maximum(m_old, jnp.max(s, axis=0, keepdims=True))
            m_ref[c] = m_new
            p = jnp.exp2(s - m_new).astype(BF16)
            acc_ref[c] = jnp.exp2(m_old - m_new) * acc_ref[c] + jnp.dot(
                v_t, p, preferred_element_type=F32)

    scores(0, sa_ref)

    def quad(t, carry):
        scores(4 * t + 1, sb_ref)
        update(sa_ref, 4 * t)
        scores(4 * t + 2, sa_ref)
        update(sb_ref, 4 * t + 1)
        scores(4 * t + 3, sb_ref)
        update(sa_ref, 4 * t + 2)
        scores(4 * t + 4, sa_ref)
        update(sb_ref, 4 * t + 3)
        return carry
    lax.fori_loop(0, (i + 3) // 4, quad, 0)

    out_t = jnp.concatenate(
        [acc_ref[c, 0:dh, :] / acc_ref[c, dh:dh + 1, :] for c in range(n_col)], axis=1)
    for g in range(group):
        o_ref[:, g * dh:(g + 1) * dh] = out_t[:, g * blk:(g + 1) * blk].T.astype(o_ref.dtype)


def _moba_attention(q, k, v):
    h, s, dh = q.shape
    hkv = k.shape[0]
    group = h // hkv
    blk = MOBA_BLOCK
    nb = s // blk
    rows = group * blk
    n_col = rows // MXU_COLS
    assert nb <= LANES and dh == LANES
    block_of_key = jnp.arange(s, dtype=jnp.int32)[:, None] // blk
    onehot = (block_of_key == jnp.arange(LANES, dtype=jnp.int32)[None, :]).astype(BF16)
    k_aug = jnp.concatenate([k, jnp.broadcast_to(onehot, (hkv, s, LANES))], axis=-1)
    v_t = v.reshape(hkv, nb, blk, dh).transpose(0, 1, 3, 2)
    ones_rows = jnp.zeros((hkv, nb, BF16_SUBLANES, blk), BF16).at[:, :, 0, :].set(1.0)
    vt_aug = jnp.concatenate([v_t, ones_rows], axis=2)
    vrows = dh + BF16_SUBLANES
    return pl.pallas_call(
        _moba_kernel,
        grid=(hkv, nb),
        in_specs=[
            pl.BlockSpec((group, blk, dh), lambda hh, i: (hh, i, 0)),
            pl.BlockSpec((None, s, 2 * LANES), lambda hh, i: (hh, 0, 0)),
            pl.BlockSpec((None, nb, vrows, blk), lambda hh, i: (hh, 0, 0, 0)),
        ],
        out_specs=pl.BlockSpec((blk, group * dh), lambda hh, i: (i, hh)),
        out_shape=jax.ShapeDtypeStruct((s, h * dh), BF16),
        scratch_shapes=[
            pltpu.VMEM((nb, dh), F32),
            pltpu.VMEM((n_col, 2 * LANES, MXU_COLS), BF16),
            pltpu.VMEM((n_col, blk, MXU_COLS), F32),
            pltpu.VMEM((n_col, blk, MXU_COLS), F32),
            pltpu.VMEM((n_col, 1, MXU_COLS), F32),
            pltpu.VMEM((n_col, vrows, MXU_COLS), F32),
        ],
        compiler_params=_params(
            ("arbitrary", "arbitrary"),
            rows * dh * 2 + s * 2 * LANES * 2 + nb * vrows * blk * 2 + blk * group * dh * 2,
            (2 * blk + vrows + LANES + 1) * rows * 4 + 6 * blk * rows * 4),
        name="moba_attention",
    )(q, k_aug, vt_aug)


def _swa_kernel(q_ref, kp_ref, ko_ref, vp_ref, vo_ref, sink_ref, bias_ref, o_ref, *, nq):
    pairs, span, _ = q_ref.shape
    blk = span // nq
    rows = pairs * blk
    odd = (pl.program_id(0) % 2) == 1
    first_step = pl.program_id(1) == 0

    lane = lax.broadcasted_iota(jnp.int32, (span + blk, LANES), 1)
    lo = lane < B_HEAD_DIM

    def strip(prev_ref, own_ref):
        x = jnp.concatenate([prev_ref[...], own_ref[...]], axis=0)
        swapped = pltpu.roll(x, B_HEAD_DIM, axis=1)
        mine_lo = jnp.where(odd, swapped, x)
        mine_hi = jnp.where(odd, x, swapped)
        return mine_lo, mine_hi

    k_lo, k_hi = strip(kp_ref, ko_ref)
    v_lo, v_hi = strip(vp_ref, vo_ref)
    zero = jnp.zeros_like(k_lo)
    k_both = jnp.where(lo, k_lo, k_hi)
    v_first = jnp.where(lo, v_lo, zero)
    v_second = jnp.where(lo, zero, v_hi)

    qlane = lax.broadcasted_iota(jnp.int32, (rows, LANES), 1)
    q_is_first = qlane < B_HEAD_DIM

    for u in range(nq):
        band = slice(u * blk, (u + 2) * blk)
        q = q_ref[:, u * blk:(u + 1) * blk, :].reshape(rows, LANES)
        qzero = jnp.zeros_like(q)
        bias = bias_ref[1] if u else bias_ref[jnp.where(first_step, 0, 1)]

        def half_head(q_half, v_half, sink):
            s = lax.dot_general(k_both[band], q_half, _NT, preferred_element_type=F32)
            s = s + bias
            m = jnp.maximum(jnp.max(s, axis=0, keepdims=True), sink)
            p = jnp.exp2(s - m)
            denom = jnp.sum(p, axis=0, keepdims=True) + jnp.exp2(sink - m)
            o_t = lax.dot_general(v_half[band], p.astype(BF16), _TN, preferred_element_type=F32)
            return o_t / denom

        out_t = (half_head(jnp.where(q_is_first, q, qzero), v_first, sink_ref[0:1, :])
                 + half_head(jnp.where(q_is_first, qzero, q), v_second, sink_ref[1:2, :]))
        for pr in range(pairs):
            o_ref[u * blk:(u + 1) * blk, pr * LANES:(pr + 1) * LANES] = (
                out_t[:, pr * blk:(pr + 1) * blk].T.astype(o_ref.dtype))


def _swa_attention(q, k, v, sinks):
    npair, s, _ = q.shape
    hkv = k.shape[1] // B_HEAD_DIM
    pairs = npair // hkv
    blk = SWA_BLOCK
    nq = SWA_STEP_BLOCKS
    span = nq * blk
    rows = pairs * blk
    sink_rows = jnp.repeat(
        (sinks.astype(F32) * LOG2_E).reshape(hkv, pairs, 2).transpose(0, 2, 1), blk, axis=-1)
    kk = np.arange(2 * blk)[:, None]
    tt = np.arange(rows)[None, :] % blk
    in_window = (kk > tt) & (kk <= tt + blk)
    band_bias = jnp.asarray(
        np.where(np.stack([in_window & (kk >= blk), in_window]), 0.0, NEG), dtype=F32)
    prev = pl.BlockSpec((blk, LANES), lambda hh, n: (jnp.maximum(n * nq - 1, 0), hh // 2))
    own = pl.BlockSpec((span, LANES), lambda hh, n: (n, hh // 2))
    return pl.pallas_call(
        functools.partial(_swa_kernel, nq=nq),
        grid=(hkv, s // span),
        in_specs=[
            pl.BlockSpec((pairs, span, LANES), lambda hh, n: (hh, n, 0)),
            prev, own, prev, own,
            pl.BlockSpec((None, 2, rows), lambda hh, n: (hh, 0, 0)),
            pl.BlockSpec((2, 2 * blk, rows), lambda hh, n: (0, 0, 0)),
        ],
        out_specs=pl.BlockSpec((span, pairs * LANES), lambda hh, n: (n, hh)),
        out_shape=jax.ShapeDtypeStruct((s, npair * LANES), BF16),
        compiler_params=_params(
            ("parallel", "parallel"),
            2 * pairs * span * LANES * 2 + 2 * (span + blk) * LANES * 2 + 2 * rows * 4
            + 4 * blk * rows * 4,
            12 * blk * rows * 4),
        name="swa_attention",
    )(q, k, k, v, v, sink_rows, band_bias)


def kernel(x, ffn_norm, ffn_w_in, ffn_w_out, attn_norm, a_w_qkv, a_w_o, kv_norm,
           w_kv_shared, b_w_q, b_sinks, b_w_o, final_norm):
    b, s, d = x.shape
    depth = ffn_norm.shape[0]
    n_a = a_w_qkv.shape[0]
    a_q = a_w_o.shape[1]
    a_kv = (a_w_qkv.shape[2] - a_q) // 2
    b_kv = w_kv_shared.shape[1] // 2
    rope_a = _rope_tables(s, A_HEAD_DIM)
    rope_b = _rope_tables(s, B_HEAD_DIM)
    w_in, w_out = ffn_w_in, ffn_w_out.astype(BF16)
    w_qkv, w_ao = a_w_qkv.astype(BF16), a_w_o.astype(BF16)
    w_kv, w_bq, w_bo = w_kv_shared.astype(BF16), b_w_q.astype(BF16), b_w_o.astype(BF16)

    def gains_entering(layer):
        if layer == depth:
            return []
        return ([kv_norm] if layer == n_a else []) + [ffn_norm[layer, 0]]

    outs = []
    for bi in range(b):
        xs = x.reshape(s, d) if b == 1 else x[bi]
        sumsq, hs = _prenorm(xs, jnp.stack(gains_entering(0)))
        k_sh = v_sh = None
        for layer in range(depth):
            hs = list(hs)
            if layer == n_a:
                h_kv = hs.pop(0)
                k_sh = _proj(h_kv, sumsq, w_kv, 0, b_kv, rope=rope_b, half=B_HEAD_DIM // 2)
                v_sh = _proj(h_kv, sumsq, w_kv, b_kv, b_kv)
            act = _swiglu_up(hs[0], sumsq, w_in, (layer, 0))
            xs, sumsq, (h,) = _matmul_residual(act, w_out, (layer, 0), xs, 0.5, 512, 512,
                                               [attn_norm[layer]])
            if layer < n_a:
                w = w_qkv[layer]
                q = _proj(h, sumsq, w, 0, a_q, rope=rope_a, half=A_HEAD_DIM // 2,
                          scale=A_HEAD_DIM ** -0.5 * LOG2_E, group_major=True)
                k = _proj(h, sumsq, w, a_q, a_kv, rope=rope_a, half=A_HEAD_DIM // 2,
                          group_major=True)
                v = _proj(h, sumsq, w, a_q + a_kv, a_kv, group_major=True)
                o = _moba_attention(q, k, v)
                w_o, lead = w_ao, (layer,)
            else:
                li = layer - n_a
                q = _proj(h, sumsq, w_bq[li], 0, w_bq.shape[2], rope=rope_b,
                          half=B_HEAD_DIM // 2, scale=B_HEAD_DIM ** -0.5 * LOG2_E, group_major=True)
                o = _swa_attention(q, k_sh, v_sh, b_sinks[li])
                w_o, lead = w_bo, (li,)
            xs, sumsq, (h,) = _matmul_residual(o, w_o, lead, xs, 1.0, 1024, 512,
                                               [ffn_norm[layer, 1]])
            act = _swiglu_up(h, sumsq, w_in, (layer, 1))
            xs, sumsq, hs = _matmul_residual(act, w_out, (layer, 1), xs, 0.5, 512, 512,
                                             gains_entering(layer + 1))
        outs.append(_final_norm(xs, sumsq, final_norm))
    return outs[0].reshape(b, s, d) if b == 1 else jnp.stack(outs, axis=0)
```

```python
import functools

import jax
import jax.numpy as jnp
import numpy as np
from jax import lax
from jax.experimental import pallas as pl
from jax.experimental.pallas import tpu as pltpu

F32 = jnp.float32
BF16 = jnp.bfloat16

RMS_EPS = 1e-6
ROPE_THETA = 10000.0
NEG = -1e30
LOG2_E = 1.4426950408889634
TAKEN = -3e38

A_HEAD_DIM = 128
A_GROUP = 4
MOBA_BLOCK = 256
MOBA_TOPK = 3

B_HEAD_DIM = 64
B_GROUP = 8
SWA_BLOCK = 128
SWA_STEP_BLOCKS = 4

LANES = 128
MXU_COLS = 256
UPDATE_COLS = 2 * MXU_COLS
BF16_SUBLANES = 16
UP_TILE_N = 256
V7X_VMEM_CAP = 60000 * 1024

_NT = (((1,), (1,)), ((), ()))
_TN = (((0,), (0,)), ((), ()))


def _params(semantics, block_bytes, scratch_bytes=0):
    need = 2 * block_bytes + scratch_bytes
    return pltpu.CompilerParams(
        dimension_semantics=semantics,
        vmem_limit_bytes=int(min(V7X_VMEM_CAP, max(need, 16 * 1024 * 1024))),
    )


def _lane_tile_sum(x):
    total = x[:, 0:LANES]
    for t in range(1, x.shape[1] // LANES):
        total = total + x[:, t * LANES:(t + 1) * LANES]
    return total


def _row_factor(sumsq_ref, width):
    mean = jnp.sum(sumsq_ref[...], axis=-1, keepdims=True) * (1.0 / width)
    return jnp.broadcast_to(lax.rsqrt(mean + RMS_EPS), sumsq_ref.shape)


def _prenorm_kernel(x_ref, g_ref, sumsq_ref, *o_refs):
    x = x_ref[...]
    sumsq_ref[...] = _lane_tile_sum(x * x)
    for n, o_ref in enumerate(o_refs):
        o_ref[...] = (x * g_ref[n:n + 1, :]).astype(o_ref.dtype)


def _prenorm(x, gains):
    s, d = x.shape
    n = gains.shape[0]
    tm = min(512, s)
    row = pl.BlockSpec((tm, d), lambda i: (i, 0))
    outs = pl.pallas_call(
        _prenorm_kernel,
        grid=(s // tm,),
        in_specs=[row, pl.BlockSpec((n, d), lambda i: (0, 0))],
        out_specs=[pl.BlockSpec((tm, LANES), lambda i: (i, 0))] + [row] * n,
        out_shape=[jax.ShapeDtypeStruct((s, LANES), F32)] + [jax.ShapeDtypeStruct((s, d), BF16)] * n,
        compiler_params=_params(("parallel",), tm * d * (4 + 2 * n), tm * d * 4),
        name="prenorm",
    )(x, gains)
    return outs[0], outs[1:]


def _final_norm_kernel(x_ref, sumsq_ref, g_ref, o_ref):
    mean = jnp.sum(sumsq_ref[...], axis=-1, keepdims=True) * (1.0 / x_ref.shape[1])
    o_ref[...] = x_ref[...] * lax.rsqrt(mean + RMS_EPS) * g_ref[...]


def _final_norm(x, sumsq, gain):
    s, d = x.shape
    tm = min(512, s)
    row = pl.BlockSpec((tm, d), lambda i: (i, 0))
    return pl.pallas_call(
        _final_norm_kernel,
        grid=(s // tm,),
        in_specs=[row, pl.BlockSpec((tm, LANES), lambda i: (i, 0)),
                  pl.BlockSpec((1, d), lambda i: (0, 0))],
        out_specs=row,
        out_shape=jax.ShapeDtypeStruct((s, d), F32),
        compiler_params=_params(("parallel",), tm * d * 8, tm * d * 4),
        name="final_norm",
    )(x, sumsq, gain[None, :])


def _swiglu_up_kernel(h_ref, sumsq_ref, wg_ref, wu_ref, o_ref, r_ref):
    @pl.when(pl.program_id(1) == 0)
    def _row_factors():
        r_ref[...] = _row_factor(sumsq_ref, h_ref.shape[1])

    h = h_ref[...]
    r = r_ref[...]
    g = jnp.dot(h, wg_ref[...].astype(BF16), preferred_element_type=F32)
    u = jnp.dot(h, wu_ref[...].astype(BF16), preferred_element_type=F32)
    for t in range(g.shape[1] // LANES):
        lanes = slice(t * LANES, (t + 1) * LANES)
        gt = g[:, lanes] * r
        o_ref[:, lanes] = (gt * (1.0 / (1.0 + jnp.exp(-gt))) * (u[:, lanes] * r)).astype(o_ref.dtype)


def _swiglu_up(h, sumsq, w_in, lead):
    s, d = h.shape
    f = w_in.shape[-1] // 2
    tm = min(1024, s)
    tn = UP_TILE_N
    nj = f // tn
    squeezed = (None,) * len(lead)
    return pl.pallas_call(
        _swiglu_up_kernel,
        grid=(s // tm, nj),
        in_specs=[
            pl.BlockSpec((tm, d), lambda i, j: (i, 0)),
            pl.BlockSpec((tm, LANES), lambda i, j: (i, 0)),
            pl.BlockSpec(squeezed + (d, tn), lambda i, j: lead + (0, j)),
            pl.BlockSpec(squeezed + (d, tn), lambda i, j: lead + (0, j + nj)),
        ],
        out_specs=pl.BlockSpec((tm, tn), lambda i, j: (i, j)),
        out_shape=jax.ShapeDtypeStruct((s, f), BF16),
        scratch_shapes=[pltpu.VMEM((tm, LANES), F32)],
        compiler_params=_params(
            ("parallel", "arbitrary"),
            tm * d * 2 + 2 * d * tn * w_in.dtype.itemsize + tm * tn * 2,
            4 * tm * tn * 4 + 2 * d * tn * 2),
        name="swiglu_up",
    )(h, sumsq, w_in, w_in)


def _matmul_residual_kernel(a_ref, w_ref, r_ref, *refs, alpha, n_gains):
    if n_gains:
        g_ref, o_ref, sumsq_ref, *h_refs = refs
    else:
        o_ref, sumsq_ref = refs
        h_refs = []
    acc = jnp.dot(a_ref[...], w_ref[...].astype(BF16), preferred_element_type=F32)
    x_new = r_ref[...] + alpha * acc
    o_ref[...] = x_new
    part = _lane_tile_sum(x_new * x_new)

    @pl.when(pl.program_id(1) == 0)
    def _first():
        sumsq_ref[...] = part

    @pl.when(pl.program_id(1) != 0)
    def _rest():
        sumsq_ref[...] += part

    for n, h_ref in enumerate(h_refs):
        h_ref[...] = (x_new * g_ref[n:n + 1, :]).astype(h_ref.dtype)


def _matmul_residual(a, w, lead, res, alpha, tm, tn, next_gains):
    s, k = a.shape
    n = w.shape[-1]
    ng = len(next_gains)
    tm, tn = min(tm, s), min(tn, n)
    squeezed = (None,) * len(lead)
    tile = pl.BlockSpec((tm, tn), lambda i, j: (i, j))
    in_specs = [
        pl.BlockSpec((tm, k), lambda i, j: (i, 0)),
        pl.BlockSpec(squeezed + (k, tn), lambda i, j: lead + (0, j)),
        tile,
    ]
    args = [a, w, res]
    if ng:
        in_specs.append(pl.BlockSpec((ng, tn), lambda i, j: (0, j)))
        args.append(jnp.stack(next_gains))
    outs = pl.pallas_call(
        functools.partial(_matmul_residual_kernel, alpha=alpha, n_gains=ng),
        grid=(s // tm, n // tn),
        in_specs=in_specs,
        out_specs=[tile, pl.BlockSpec((tm, LANES), lambda i, j: (i, 0))] + [tile] * ng,
        out_shape=[jax.ShapeDtypeStruct((s, n), F32), jax.ShapeDtypeStruct((s, LANES), F32)]
        + [jax.ShapeDtypeStruct((s, n), BF16)] * ng,
        compiler_params=_params(
            ("parallel", "arbitrary"),
            tm * k * 2 + k * tn * w.dtype.itemsize + 2 * tm * tn * 4 + tm * LANES * 4
            + ng * tm * tn * 2,
            3 * tm * tn * 4 + k * tn * 2),
        name="matmul_residual",
    )(*args)
    return outs[0], outs[1], outs[2:]


def _rope_tables(seq, dim):
    half = dim // 2
    inv = 1.0 / (ROPE_THETA ** (jnp.arange(0, dim, 2, dtype=F32) / dim))
    ang = jnp.arange(seq, dtype=F32)[:, None] * inv[None, :]
    cos, sin = jnp.cos(ang), jnp.sin(ang)
    zero = jnp.zeros_like(sin)
    reps = LANES // dim
    cos_t = jnp.tile(jnp.concatenate([cos, cos], axis=-1), (1, reps))
    s_lo = jnp.tile(jnp.concatenate([-sin, zero], axis=-1), (1, reps))
    s_hi = jnp.tile(jnp.concatenate([zero, sin], axis=-1), (1, reps))
    return cos_t, s_lo, s_hi


def _proj_kernel(h_ref, sumsq_ref, w_ref, *refs, half, scale, group_major):
    acc = jnp.dot(h_ref[...], w_ref[...].astype(BF16), preferred_element_type=F32)
    if half:
        cos_ref, slo_ref, shi_ref, o_ref = refs
        cos, s_lo, s_hi = cos_ref[...], slo_ref[...], shi_ref[...]
    else:
        (o_ref,) = refs
    r = _row_factor(sumsq_ref, h_ref.shape[1]) * scale
    for g in range(acc.shape[1] // LANES):
        x = acc[:, g * LANES:(g + 1) * LANES] * r
        if half:
            x = (x * cos + pltpu.roll(x, LANES - half, axis=1) * s_lo
                 + pltpu.roll(x, half, axis=1) * s_hi)
        if group_major:
            o_ref[g] = x.astype(o_ref.dtype)
        else:
            o_ref[:, g * LANES:(g + 1) * LANES] = x.astype(o_ref.dtype)


def _proj(h, sumsq, w, col0, ncols, *, rope=None, half=0, scale=1.0, group_major=False):
    s, k = h.shape
    tm = min(1024, s)
    tn = min(512, ncols)
    j0 = col0 // tn
    in_specs = [
        pl.BlockSpec((tm, k), lambda i, j: (i, 0)),
        pl.BlockSpec((tm, LANES), lambda i, j: (i, 0)),
        pl.BlockSpec((k, tn), lambda i, j: (0, j + j0)),
    ]
    args = [h, sumsq, w]
    if half:
        tab = pl.BlockSpec((tm, LANES), lambda i, j: (i, 0))
        in_specs += [tab, tab, tab]
        args += list(rope)
    if group_major:
        gpt = tn // LANES
        out_spec = pl.BlockSpec((gpt, tm, LANES), lambda i, j: (j, i, 0))
        out_shape = jax.ShapeDtypeStruct((ncols // LANES, s, LANES), BF16)
    else:
        out_spec = pl.BlockSpec((tm, tn), lambda i, j: (i, j))
        out_shape = jax.ShapeDtypeStruct((s, ncols), BF16)
    return pl.pallas_call(
        functools.partial(_proj_kernel, half=half, scale=scale, group_major=group_major),
        grid=(s // tm, ncols // tn),
        in_specs=in_specs,
        out_specs=out_spec,
        out_shape=out_shape,
        compiler_params=_params(
            ("parallel", "arbitrary"),
            tm * k * 2 + k * tn * w.dtype.itemsize + 4 * tm * LANES * 4 + tm * tn * 2,
            3 * tm * tn * 4 + k * tn * 2),
        name="proj",
    )(*args)


def _moba_kernel(q_ref, k_ref, vt_ref, o_ref, kmean_ref, qa_ref, sa_ref, sb_ref, m_ref, acc_ref):
    group, blk, dh = q_ref.shape
    nb = vt_ref.shape[0]
    rows = group * blk
    i = pl.program_id(1)

    @pl.when(i == 0)
    def _block_means():
        def body(j, carry):
            kj = k_ref[pl.ds(pl.multiple_of(j * blk, blk), blk), 0:dh].astype(F32)
            kmean_ref[pl.ds(j, 1), :] = jnp.sum(kj, axis=0, keepdims=True) * (1.0 / blk)
            return carry
        lax.fori_loop(0, nb, body, 0)

    q_t = q_ref[...].reshape(rows, dh).astype(F32).T.astype(BF16)
    qa_ref[0:dh, :] = q_t

    km = kmean_ref[...]
    km_hi = km.astype(BF16)
    km_lo = (km - km_hi.astype(F32)).astype(BF16)
    gate = (jnp.dot(km_hi, q_t, preferred_element_type=F32)
            + jnp.dot(km_lo, q_t, preferred_element_type=F32))
    bidx = lax.broadcasted_iota(jnp.int32, (nb, rows), 0)
    past = bidx < i
    bidx = bidx.astype(F32)
    gate = jnp.where(past, gate, NEG)
    bias = jnp.full((nb, rows), NEG, F32)
    for _ in range(min(MOBA_TOPK, nb)):
        top = jnp.max(gate, axis=0, keepdims=True)
        first = jnp.min(jnp.where(gate == top, bidx, float(nb)), axis=0, keepdims=True)
        pick = bidx == first
        bias = jnp.where(pick & past, 0.0, bias)
        gate = jnp.where(pick, TAKEN, gate)
    qa_ref[dh:dh + nb, :] = bias.astype(BF16)
    if nb < LANES:
        qa_ref[dh + nb:dh + LANES, :] = jnp.zeros((LANES - nb, rows), BF16)

    own = pl.multiple_of(i * blk, blk)
    s = jnp.dot(k_ref[pl.ds(own, blk), 0:dh], q_t, preferred_element_type=F32)
    kpos = lax.broadcasted_iota(jnp.int32, (blk, rows), 0)
    qpos = lax.broadcasted_iota(jnp.int32, (blk, rows), 1) & (blk - 1)
    s = jnp.where(kpos <= qpos, s, NEG)
    m = jnp.max(s, axis=0, keepdims=True)
    m_ref[...] = m
    acc_ref[...] = jnp.dot(vt_ref[i], jnp.exp2(s - m).astype(BF16), preferred_element_type=F32)

    def scores(j, s_ref):
        off = pl.multiple_of(jnp.minimum(j, nb - 1) * blk, blk)
        s_ref[...] = jnp.dot(k_ref[pl.ds(off, blk), :], qa_ref[...], preferred_element_type=F32)

    reread = pl.multiple_of(jnp.minimum(i, 0) * blk, blk)

    def update(s_ref, j):
        v_t = vt_ref[jnp.minimum(j, nb - 1)]
        for c in range(rows // UPDATE_COLS):
            cols = slice(c * UPDATE_COLS, (c + 1) * UPDATE_COLS)
            s = s_ref[pl.ds(reread, blk), cols]
            m_old = m_ref[:, cols]
            m_new = jnp.maximum(m_old, jnp.max(s, axis=0, keepdims=True))
            m_ref[:, cols] = m_new
            p = jnp.exp2(s - m_new).astype(BF16)
            acc_ref[:, cols] = jnp.exp2(m_old - m_new) * acc_ref[:, cols] + jnp.dot(
                v_t, p, preferred_element_type=F32)

    scores(0, sa_ref)

    def quad(t, carry):
        scores(4 * t + 1, sb_ref)
        update(sa_ref, 4 * t)
        scores(4 * t + 2, sa_ref)
        update(sb_ref, 4 * t + 1)
        scores(4 * t + 3, sb_ref)
        update(sa_ref, 4 * t + 2)
        scores(4 * t + 4, sa_ref)
        update(sb_ref, 4 * t + 3)
        return carry
    lax.fori_loop(0, (i + 3) // 4, quad, 0)

    out_t = acc_ref[0:dh, :] / acc_ref[dh:dh + 1, :]
    for g in range(group):
        o_ref[:, g * dh:(g + 1) * dh] = out_t[:, g * blk:(g + 1) * blk].T.astype(o_ref.dtype)


def _moba_attention(q, k, v):
    h, s, dh = q.shape
    hkv = k.shape[0]
    group = h // hkv
    blk = MOBA_BLOCK
    nb = s // blk
    rows = group * blk
    assert nb <= LANES and dh == LANES
    block_of_key = jnp.arange(s, dtype=jnp.int32)[:, None] // blk
    onehot = (block_of_key == jnp.arange(LANES, dtype=jnp.int32)[None, :]).astype(BF16)
    k_aug = jnp.concatenate([k, jnp.broadcast_to(onehot, (hkv, s, LANES))], axis=-1)
    v_t = v.reshape(hkv, nb, blk, dh).transpose(0, 1, 3, 2)
    ones_rows = jnp.zeros((hkv, nb, BF16_SUBLANES, blk), BF16).at[:, :, 0, :].set(1.0)
    vt_aug = jnp.concatenate([v_t, ones_rows], axis=2)
    vrows = dh + BF16_SUBLANES
    return pl.pallas_call(
        _moba_kernel,
        grid=(hkv, nb),
        in_specs=[
            pl.BlockSpec((group, blk, dh), lambda hh, i: (hh, i, 0)),
            pl.BlockSpec((None, s, 2 * LANES), lambda hh, i: (hh, 0, 0)),
            pl.BlockSpec((None, nb, vrows, blk), lambda hh, i: (hh, 0, 0, 0)),
        ],
        out_specs=pl.BlockSpec((blk, group * dh), lambda hh, i: (i, hh)),
        out_shape=jax.ShapeDtypeStruct((s, h * dh), BF16),
        scratch_shapes=[
            pltpu.VMEM((nb, dh), F32),
            pltpu.VMEM((2 * LANES, rows), BF16),
            pltpu.VMEM((blk, rows), F32),
            pltpu.VMEM((blk, rows), F32),
            pltpu.VMEM((1, rows), F32),
            pltpu.VMEM((vrows, rows), F32),
        ],
        compiler_params=_params(
            ("arbitrary", "arbitrary"),
            rows * dh * 2 + s * 2 * LANES * 2 + nb * vrows * blk * 2 + blk * group * dh * 2,
            (2 * blk + vrows + LANES + 1) * rows * 4 + 6 * blk * rows * 4),
        name="moba_attention",
    )(q, k_aug, vt_aug)


def _swa_kernel(q_ref, kp_ref, ko_ref, vp_ref, vo_ref, sink_ref, bias_ref, o_ref, *, nq):
    pairs, span, _ = q_ref.shape
    blk = span // nq
    rows = pairs * blk
    odd = (pl.program_id(0) % 2) == 1
    first_step = pl.program_id(1) == 0

    lane = lax.broadcasted_iota(jnp.int32, (span + blk, LANES), 1)
    lo = lane < B_HEAD_DIM

    def strip(prev_ref, own_ref):
        x = jnp.concatenate([prev_ref[...], own_ref[...]], axis=0)
        swapped = pltpu.roll(x, B_HEAD_DIM, axis=1)
        mine_lo = jnp.where(odd, swapped, x)
        mine_hi = jnp.where(odd, x, swapped)
        return mine_lo, mine_hi

    k_lo, k_hi = strip(kp_ref, ko_ref)
    v_lo, v_hi = strip(vp_ref, vo_ref)
    zero = jnp.zeros_like(k_lo)
    k_both = jnp.where(lo, k_lo, k_hi)
    v_first = jnp.where(lo, v_lo, zero)
    v_second = jnp.where(lo, zero, v_hi)

    qlane = lax.broadcasted_iota(jnp.int32, (rows, LANES), 1)
    q_is_first = qlane < B_HEAD_DIM

    chains = []
    for u in range(nq):
        band = slice(u * blk, (u + 2) * blk)
        q = q_ref[:, u * blk:(u + 1) * blk, :].reshape(rows, LANES)
        qzero = jnp.zeros_like(q)
        bias = bias_ref[1] if u else bias_ref[jnp.where(first_step, 0, 1)]
        for q_half, v_half, sink in (
                (jnp.where(q_is_first, q, qzero), v_first, sink_ref[0:1, :]),
                (jnp.where(q_is_first, qzero, q), v_second, sink_ref[1:2, :])):
            s = lax.dot_general(k_both[band], q_half, _NT, preferred_element_type=F32)
            chains.append((s + bias, v_half[band], sink))
    probs = []
    for s, v_band, sink in chains:
        m = jnp.maximum(jnp.max(s, axis=0, keepdims=True), sink)
        p = jnp.exp2(s - m)
        denom = jnp.sum(p, axis=0, keepdims=True) + jnp.exp2(sink - m)
        probs.append((p.astype(BF16), v_band, denom))
    outs = [lax.dot_general(v_band, p, _TN, preferred_element_type=F32) / denom
            for p, v_band, denom in probs]
    for u in range(nq):
        out_t = outs[2 * u] + outs[2 * u + 1]
        for pr in range(pairs):
            o_ref[u * blk:(u + 1) * blk, pr * LANES:(pr + 1) * LANES] = (
                out_t[:, pr * blk:(pr + 1) * blk].T.astype(o_ref.dtype))


def _swa_attention(q, k, v, sinks):
    npair, s, _ = q.shape
    hkv = k.shape[1] // B_HEAD_DIM
    pairs = npair // hkv
    blk = SWA_BLOCK
    nq = SWA_STEP_BLOCKS
    span = nq * blk
    rows = pairs * blk
    sink_rows = jnp.repeat(
        (sinks.astype(F32) * LOG2_E).reshape(hkv, pairs, 2).transpose(0, 2, 1), blk, axis=-1)
    kk = np.arange(2 * blk)[:, None]
    tt = np.arange(rows)[None, :] % blk
    in_window = (kk > tt) & (kk <= tt + blk)
    band_bias = jnp.asarray(
        np.where(np.stack([in_window & (kk >= blk), in_window]), 0.0, NEG), dtype=F32)
    prev = pl.BlockSpec((blk, LANES), lambda hh, n: (jnp.maximum(n * nq - 1, 0), hh // 2))
    own = pl.BlockSpec((span, LANES), lambda hh, n: (n, hh // 2))
    return pl.pallas_call(
        functools.partial(_swa_kernel, nq=nq),
        grid=(hkv, s // span),
        in_specs=[
            pl.BlockSpec((pairs, span, LANES), lambda hh, n: (hh, n, 0)),
            prev, own, prev, own,
            pl.BlockSpec((None, 2, rows), lambda hh, n: (hh, 0, 0)),
            pl.BlockSpec((2, 2 * blk, rows), lambda hh, n: (0, 0, 0)),
        ],
        out_specs=pl.BlockSpec((span, pairs * LANES), lambda hh, n: (n, hh)),
        out_shape=jax.ShapeDtypeStruct((s, npair * LANES), BF16),
        compiler_params=_params(
            ("parallel", "parallel"),
            2 * pairs * span * LANES * 2 + 2 * (span + blk) * LANES * 2 + 2 * rows * 4
            + 4 * blk * rows * 4,
            12 * blk * rows * 4),
        name="swa_attention",
    )(q, k, k, v, v, sink_rows, band_bias)


def kernel(x, ffn_norm, ffn_w_in, ffn_w_out, attn_norm, a_w_qkv, a_w_o, kv_norm,
           w_kv_shared, b_w_q, b_sinks, b_w_o, final_norm):
    b, s, d = x.shape
    depth = ffn_norm.shape[0]
    n_a = a_w_qkv.shape[0]
    a_q = a_w_o.shape[1]
    a_kv = (a_w_qkv.shape[2] - a_q) // 2
    b_kv = w_kv_shared.shape[1] // 2
    rope_a = _rope_tables(s, A_HEAD_DIM)
    rope_b = _rope_tables(s, B_HEAD_DIM)
    w_in, w_out = ffn_w_in, ffn_w_out.astype(BF16)
    w_qkv, w_ao = a_w_qkv.astype(BF16), a_w_o.astype(BF16)
    w_kv, w_bq, w_bo = w_kv_shared.astype(BF16), b_w_q.astype(BF16), b_w_o.astype(BF16)

    def gains_entering(layer):
        if layer == depth:
            return []
        return ([kv_norm] if layer == n_a else []) + [ffn_norm[layer, 0]]

    outs = []
    for bi in range(b):
        xs = x.reshape(s, d) if b == 1 else x[bi]
        sumsq, hs = _prenorm(xs, jnp.stack(gains_entering(0)))
        k_sh = v_sh = None
        for layer in range(depth):
            hs = list(hs)
            if layer == n_a:
                h_kv = hs.pop(0)
                k_sh = _proj(h_kv, sumsq, w_kv, 0, b_kv, rope=rope_b, half=B_HEAD_DIM // 2)
                v_sh = _proj(h_kv, sumsq, w_kv, b_kv, b_kv)
            act = _swiglu_up(hs[0], sumsq, w_in, (layer, 0))
            xs, sumsq, (h,) = _matmul_residual(act, w_out, (layer, 0), xs, 0.5, 512, 512,
                                               [attn_norm[layer]])
            if layer < n_a:
                w = w_qkv[layer]
                q = _proj(h, sumsq, w, 0, a_q, rope=rope_a, half=A_HEAD_DIM // 2,
                          scale=A_HEAD_DIM ** -0.5 * LOG2_E, group_major=True)
                k = _proj(h, sumsq, w, a_q, a_kv, rope=rope_a, half=A_HEAD_DIM // 2,
                          group_major=True)
                v = _proj(h, sumsq, w, a_q + a_kv, a_kv, group_major=True)
                o = _moba_attention(q, k, v)
                w_o, lead = w_ao, (layer,)
            else:
                li = layer - n_a
                q = _proj(h, sumsq, w_bq[li], 0, w_bq.shape[2], rope=rope_b,
                          half=B_HEAD_DIM // 2, scale=B_HEAD_DIM ** -0.5 * LOG2_E, group_major=True)
                o = _swa_attention(q, k_sh, v_sh, b_sinks[li])
                w_o, lead = w_bo, (li,)
            xs, sumsq, (h,) = _matmul_residual(o, w_o, lead, xs, 1.0, 1024, 512,
                                               [ffn_norm[layer, 1]])
            act = _swiglu_up(h, sumsq, w_in, (layer, 1))
            xs, sumsq, hs = _matmul_residual(act, w_out, (layer, 1), xs, 0.5, 512, 512,
                                             gains_entering(layer + 1))
        outs.append(_final_norm(xs, sumsq, final_norm))
    return outs[0].reshape(b, s, d) if b == 1 else jnp.stack(outs, axis=0)
```

```python
import functools

import jax
import jax.numpy as jnp
import numpy as np
from jax import lax
from jax.experimental import pallas as pl
from jax.experimental.pallas import tpu as pltpu

F32 = jnp.float32
BF16 = jnp.bfloat16

RMS_EPS = 1e-6
ROPE_THETA = 10000.0
NEG = -1e30
LOG2_E = 1.4426950408889634
TAKEN = -3e38

A_HEAD_DIM = 128
A_GROUP = 4
MOBA_BLOCK = 256
MOBA_TOPK = 3

B_HEAD_DIM = 64
B_GROUP = 8
SWA_BLOCK = 128
SWA_STEP_BLOCKS = 4

LANES = 128
MXU_COLS = 256
BF16_SUBLANES = 16
UP_TILE_N = 256
V7X_VMEM_CAP = 60000 * 1024

_NT = (((1,), (1,)), ((), ()))
_TN = (((0,), (0,)), ((), ()))


def _params(semantics, block_bytes, scratch_bytes=0):
    need = 2 * block_bytes + scratch_bytes
    return pltpu.CompilerParams(
        dimension_semantics=semantics,
        vmem_limit_bytes=int(min(V7X_VMEM_CAP, max(need, 16 * 1024 * 1024))),
    )


def _lane_tile_sum(x):
    total = x[:, 0:LANES]
    for t in range(1, x.shape[1] // LANES):
        total = total + x[:, t * LANES:(t + 1) * LANES]
    return total


def _row_factor(sumsq_ref, width):
    mean = jnp.sum(sumsq_ref[...], axis=-1, keepdims=True) * (1.0 / width)
    return jnp.broadcast_to(lax.rsqrt(mean + RMS_EPS), sumsq_ref.shape)


def _prenorm_kernel(x_ref, g_ref, sumsq_ref, *o_refs):
    x = x_ref[...]
    sumsq_ref[...] = _lane_tile_sum(x * x)
    for n, o_ref in enumerate(o_refs):
        o_ref[...] = (x * g_ref[n:n + 1, :]).astype(o_ref.dtype)


def _prenorm(x, gains):
    s, d = x.shape
    n = gains.shape[0]
    tm = min(512, s)
    row = pl.BlockSpec((tm, d), lambda i: (i, 0))
    outs = pl.pallas_call(
        _prenorm_kernel,
        grid=(s // tm,),
        in_specs=[row, pl.BlockSpec((n, d), lambda i: (0, 0))],
        out_specs=[pl.BlockSpec((tm, LANES), lambda i: (i, 0))] + [row] * n,
        out_shape=[jax.ShapeDtypeStruct((s, LANES), F32)] + [jax.ShapeDtypeStruct((s, d), BF16)] * n,
        compiler_params=_params(("parallel",), tm * d * (4 + 2 * n), tm * d * 4),
        name="prenorm",
    )(x, gains)
    return outs[0], outs[1:]


def _final_norm_kernel(x_ref, sumsq_ref, g_ref, o_ref):
    mean = jnp.sum(sumsq_ref[...], axis=-1, keepdims=True) * (1.0 / x_ref.shape[1])
    o_ref[...] = x_ref[...] * lax.rsqrt(mean + RMS_EPS) * g_ref[...]


def _final_norm(x, sumsq, gain):
    s, d = x.shape
    tm = min(512, s)
    row = pl.BlockSpec((tm, d), lambda i: (i, 0))
    return pl.pallas_call(
        _final_norm_kernel,
        grid=(s // tm,),
        in_specs=[row, pl.BlockSpec((tm, LANES), lambda i: (i, 0)),
                  pl.BlockSpec((1, d), lambda i: (0, 0))],
        out_specs=row,
        out_shape=jax.ShapeDtypeStruct((s, d), F32),
        compiler_params=_params(("parallel",), tm * d * 8, tm * d * 4),
        name="final_norm",
    )(x, sumsq, gain[None, :])


def _swiglu_up_kernel(h_ref, sumsq_ref, wg_ref, wu_ref, o_ref, r_ref):
    @pl.when(pl.program_id(1) == 0)
    def _row_factors():
        r_ref[...] = _row_factor(sumsq_ref, h_ref.shape[1])

    h = h_ref[...]
    r = r_ref[...]
    g = jnp.dot(h, wg_ref[...].astype(BF16), preferred_element_type=F32)
    u = jnp.dot(h, wu_ref[...].astype(BF16), preferred_element_type=F32)
    for t in range(g.shape[1] // LANES):
        lanes = slice(t * LANES, (t + 1) * LANES)
        gt = g[:, lanes] * r
        o_ref[:, lanes] = (gt * (1.0 / (1.0 + jnp.exp(-gt))) * (u[:, lanes] * r)).astype(o_ref.dtype)


def _swiglu_up(h, sumsq, w_in, lead):
    s, d = h.shape
    f = w_in.shape[-1] // 2
    tm = min(1024, s)
    tn = UP_TILE_N
    nj = f // tn
    squeezed = (None,) * len(lead)
    return pl.pallas_call(
        _swiglu_up_kernel,
        grid=(s // tm, nj),
        in_specs=[
            pl.BlockSpec((tm, d), lambda i, j: (i, 0)),
            pl.BlockSpec((tm, LANES), lambda i, j: (i, 0)),
            pl.BlockSpec(squeezed + (d, tn), lambda i, j: lead + (0, j)),
            pl.BlockSpec(squeezed + (d, tn), lambda i, j: lead + (0, j + nj)),
        ],
        out_specs=pl.BlockSpec((tm, tn), lambda i, j: (i, j)),
        out_shape=jax.ShapeDtypeStruct((s, f), BF16),
        scratch_shapes=[pltpu.VMEM((tm, LANES), F32)],
        compiler_params=_params(
            ("parallel", "arbitrary"),
            tm * d * 2 + 2 * d * tn * w_in.dtype.itemsize + tm * tn * 2,
            4 * tm * tn * 4 + 2 * d * tn * 2),
        name="swiglu_up",
    )(h, sumsq, w_in, w_in)


def _matmul_residual_kernel(a_ref, w_ref, r_ref, *refs, alpha, n_gains):
    if n_gains:
        g_ref, o_ref, sumsq_ref, *h_refs = refs
    else:
        o_ref, sumsq_ref = refs
        h_refs = []
    acc = jnp.dot(a_ref[...], w_ref[...].astype(BF16), preferred_element_type=F32)
    x_new = r_ref[...] + alpha * acc
    o_ref[...] = x_new
    part = _lane_tile_sum(x_new * x_new)

    @pl.when(pl.program_id(1) == 0)
    def _first():
        sumsq_ref[...] = part

    @pl.when(pl.program_id(1) != 0)
    def _rest():
        sumsq_ref[...] += part

    for n, h_ref in enumerate(h_refs):
        h_ref[...] = (x_new * g_ref[n:n + 1, :]).astype(h_ref.dtype)


def _matmul_residual(a, w, lead, res, alpha, tm, tn, next_gains):
    s, k = a.shape
    n = w.shape[-1]
    ng = len(next_gains)
    tm, tn = min(tm, s), min(tn, n)
    squeezed = (None,) * len(lead)
    tile = pl.BlockSpec((tm, tn), lambda i, j: (i, j))
    in_specs = [
        pl.BlockSpec((tm, k), lambda i, j: (i, 0)),
        pl.BlockSpec(squeezed + (k, tn), lambda i, j: lead + (0, j)),
        tile,
    ]
    args = [a, w, res]
    if ng:
        in_specs.append(pl.BlockSpec((ng, tn), lambda i, j: (0, j)))
        args.append(jnp.stack(next_gains))
    outs = pl.pallas_call(
        functools.partial(_matmul_residual_kernel, alpha=alpha, n_gains=ng),
        grid=(s // tm, n // tn),
        in_specs=in_specs,
        out_specs=[tile, pl.BlockSpec((tm, LANES), lambda i, j: (i, 0))] + [tile] * ng,
        out_shape=[jax.ShapeDtypeStruct((s, n), F32), jax.ShapeDtypeStruct((s, LANES), F32)]
        + [jax.ShapeDtypeStruct((s, n), BF16)] * ng,
        compiler_params=_params(
            ("parallel", "arbitrary"),
            tm * k * 2 + k * tn * w.dtype.itemsize + 2 * tm * tn * 4 + tm * LANES * 4
            + ng * tm * tn * 2,
            3 * tm * tn * 4 + k * tn * 2),
        name="matmul_residual",
    )(*args)
    return outs[0], outs[1], outs[2:]


def _rope_tables(seq, dim):
    half = dim // 2
    inv = 1.0 / (ROPE_THETA ** (jnp.arange(0, dim, 2, dtype=F32) / dim))
    ang = jnp.arange(seq, dtype=F32)[:, None] * inv[None, :]
    cos, sin = jnp.cos(ang), jnp.sin(ang)
    zero = jnp.zeros_like(sin)
    reps = LANES // dim
    cos_t = jnp.tile(jnp.concatenate([cos, cos], axis=-1), (1, reps))
    s_lo = jnp.tile(jnp.concatenate([-sin, zero], axis=-1), (1, reps))
    s_hi = jnp.tile(jnp.concatenate([zero, sin], axis=-1), (1, reps))
    return cos_t, s_lo, s_hi


def _proj_kernel(h_ref, sumsq_ref, w_ref, *refs, half, scale, group_major):
    acc = jnp.dot(h_ref[...], w_ref[...].astype(BF16), preferred_element_type=F32)
    if half:
        cos_ref, slo_ref, shi_ref, o_ref = refs
        cos, s_lo, s_hi = cos_ref[...], slo_ref[...], shi_ref[...]
    else:
        (o_ref,) = refs
    r = _row_factor(sumsq_ref, h_ref.shape[1]) * scale
    for g in range(acc.shape[1] // LANES):
        x = acc[:, g * LANES:(g + 1) * LANES] * r
        if half:
            x = (x * cos + pltpu.roll(x, LANES - half, axis=1) * s_lo
                 + pltpu.roll(x, half, axis=1) * s_hi)
        if group_major:
            o_ref[g] = x.astype(o_ref.dtype)
        else:
            o_ref[:, g * LANES:(g + 1) * LANES] = x.astype(o_ref.dtype)


def _proj(h, sumsq, w, col0, ncols, *, rope=None, half=0, scale=1.0, group_major=False):
    s, k = h.shape
    tm = min(1024, s)
    tn = min(512, ncols)
    j0 = col0 // tn
    in_specs = [
        pl.BlockSpec((tm, k), lambda i, j: (i, 0)),
        pl.BlockSpec((tm, LANES), lambda i, j: (i, 0)),
        pl.BlockSpec((k, tn), lambda i, j: (0, j + j0)),
    ]
    args = [h, sumsq, w]
    if half:
        tab = pl.BlockSpec((tm, LANES), lambda i, j: (i, 0))
        in_specs += [tab, tab, tab]
        args += list(rope)
    if group_major:
        gpt = tn // LANES
        out_spec = pl.BlockSpec((gpt, tm, LANES), lambda i, j: (j, i, 0))
        out_shape = jax.ShapeDtypeStruct((ncols // LANES, s, LANES), BF16)
    else:
        out_spec = pl.BlockSpec((tm, tn), lambda i, j: (i, j))
        out_shape = jax.ShapeDtypeStruct((s, ncols), BF16)
    return pl.pallas_call(
        functools.partial(_proj_kernel, half=half, scale=scale, group_major=group_major),
        grid=(s // tm, ncols // tn),
        in_specs=in_specs,
        out_specs=out_spec,
        out_shape=out_shape,
        compiler_params=_params(
            ("parallel", "arbitrary"),
            tm * k * 2 + k * tn * w.dtype.itemsize + 4 * tm * LANES * 4 + tm * tn * 2,
            3 * tm * tn * 4 + k * tn * 2),
        name="proj",
    )(*args)


def _moba_kernel(q_ref, k_ref, vt_ref, o_ref, kmean_ref, qa_ref, sa_ref, sb_ref, m_ref, acc_ref):
    group, blk, dh = q_ref.shape
    nb = vt_ref.shape[0]
    rows = group * blk
    i = pl.program_id(1)

    @pl.when(i == 0)
    def _block_means():
        def body(j, carry):
            kj = k_ref[pl.ds(pl.multiple_of(j * blk, blk), blk), 0:dh].astype(F32)
            kmean_ref[pl.ds(j, 1), :] = jnp.sum(kj, axis=0, keepdims=True) * (1.0 / blk)
            return carry
        lax.fori_loop(0, nb, body, 0)

    q_t = q_ref[...].reshape(rows, dh).astype(F32).T.astype(BF16)
    qa_ref[0:dh, :] = q_t

    own = pl.multiple_of(i * blk, blk)
    s_own = jnp.dot(k_ref[pl.ds(own, blk), 0:dh], q_t, preferred_element_type=F32)

    km = kmean_ref[...]
    km_hi = km.astype(BF16)
    km_lo = (km - km_hi.astype(F32)).astype(BF16)
    gate = (jnp.dot(km_hi, q_t, preferred_element_type=F32)
            + jnp.dot(km_lo, q_t, preferred_element_type=F32))
    bidx = lax.broadcasted_iota(jnp.int32, (nb, rows), 0)
    past = bidx < i
    bidx = bidx.astype(F32)
    gate = jnp.where(past, gate, NEG)
    bias = jnp.full((nb, rows), NEG, F32)
    for _ in range(min(MOBA_TOPK, nb)):
        top = jnp.max(gate, axis=0, keepdims=True)
        first = jnp.min(jnp.where(gate == top, bidx, float(nb)), axis=0, keepdims=True)
        pick = bidx == first
        bias = jnp.where(pick & past, 0.0, bias)
        gate = jnp.where(pick, TAKEN, gate)
    qa_ref[dh:dh + nb, :] = bias.astype(BF16)
    if nb < LANES:
        qa_ref[dh + nb:dh + LANES, :] = jnp.zeros((LANES - nb, rows), BF16)

    s = s_own
    kpos = lax.broadcasted_iota(jnp.int32, (blk, rows), 0)
    qpos = lax.broadcasted_iota(jnp.int32, (blk, rows), 1) & (blk - 1)
    s = jnp.where(kpos <= qpos, s, NEG)
    m = jnp.max(s, axis=0, keepdims=True)
    m_ref[...] = m
    acc_ref[...] = jnp.dot(vt_ref[i], jnp.exp2(s - m).astype(BF16), preferred_element_type=F32)

    def scores(j, s_ref):
        off = pl.multiple_of(jnp.minimum(j, nb - 1) * blk, blk)
        s_ref[...] = jnp.dot(k_ref[pl.ds(off, blk), :], qa_ref[...], preferred_element_type=F32)

    reread = pl.multiple_of(jnp.minimum(i, 0) * blk, blk)

    def update(s_ref, j):
        v_t = vt_ref[jnp.minimum(j, nb - 1)]
        for c in range(rows // MXU_COLS):
            cols = slice(c * MXU_COLS, (c + 1) * MXU_COLS)
            s = s_ref[pl.ds(reread, blk), cols]
            m_old = m_ref[:, cols]
            m_new = jnp.maximum(m_old, jnp.max(s, axis=0, keepdims=True))
            m_ref[:, cols] = m_new
            p = jnp.exp2((s - m_new).astype(BF16))
            acc_ref[:, cols] = jnp.exp2(m_old - m_new) * acc_ref[:, cols] + jnp.dot(
                v_t, p, preferred_element_type=F32)

    scores(0, sa_ref)

    def quad(t, carry):
        scores(4 * t + 1, sb_ref)
        update(sa_ref, 4 * t)
        scores(4 * t + 2, sa_ref)
        update(sb_ref, 4 * t + 1)
        scores(4 * t + 3, sb_ref)
        update(sa_ref, 4 * t + 2)
        scores(4 * t + 4, sa_ref)
        update(sb_ref, 4 * t + 3)
        return carry
    lax.fori_loop(0, (i + 3) // 4, quad, 0)

    out_t = acc_ref[0:dh, :] / acc_ref[dh:dh + 1, :]
    for g in range(group):
        o_ref[:, g * dh:(g + 1) * dh] = out_t[:, g * blk:(g + 1) * blk].T.astype(o_ref.dtype)


def _moba_attention(q, k, v):
    h, s, dh = q.shape
    hkv = k.shape[0]
    group = h // hkv
    blk = MOBA_BLOCK
    nb = s // blk
    rows = group * blk
    assert nb <= LANES and dh == LANES
    block_of_key = jnp.arange(s, dtype=jnp.int32)[:, None] // blk
    onehot = (block_of_key == jnp.arange(LANES, dtype=jnp.int32)[None, :]).astype(BF16)
    k_aug = jnp.concatenate([k, jnp.broadcast_to(onehot, (hkv, s, LANES))], axis=-1)
    v_t = v.reshape(hkv, nb, blk, dh).transpose(0, 1, 3, 2)
    ones_rows = jnp.zeros((hkv, nb, BF16_SUBLANES, blk), BF16).at[:, :, 0, :].set(1.0)
    vt_aug = jnp.concatenate([v_t, ones_rows], axis=2)
    vrows = dh + BF16_SUBLANES
    return pl.pallas_call(
        _moba_kernel,
        grid=(hkv, nb),
        in_specs=[
            pl.BlockSpec((group, blk, dh), lambda hh, i: (hh, i, 0)),
            pl.BlockSpec((None, s, 2 * LANES), lambda hh, i: (hh, 0, 0)),
            pl.BlockSpec((None, nb, vrows, blk), lambda hh, i: (hh, 0, 0, 0)),
        ],
        out_specs=pl.BlockSpec((blk, group * dh), lambda hh, i: (i, hh)),
        out_shape=jax.ShapeDtypeStruct((s, h * dh), BF16),
        scratch_shapes=[
            pltpu.VMEM((nb, dh), F32),
            pltpu.VMEM((2 * LANES, rows), BF16),
            pltpu.VMEM((blk, rows), F32),
            pltpu.VMEM((blk, rows), F32),
            pltpu.VMEM((1, rows), F32),
            pltpu.VMEM((vrows, rows), F32),
        ],
        compiler_params=_params(
            ("arbitrary", "arbitrary"),
            rows * dh * 2 + s * 2 * LANES * 2 + nb * vrows * blk * 2 + blk * group * dh * 2,
            (2 * blk + vrows + LANES + 1) * rows * 4 + 6 * blk * rows * 4),
        name="moba_attention",
    )(q, k_aug, vt_aug)


def _swa_kernel(q_ref, kp_ref, ko_ref, vp_ref, vo_ref, sink_ref, bias_ref, o_ref, *, nq):
    pairs, span, _ = q_ref.shape
    blk = span // nq
    rows = pairs * blk
    odd = (pl.program_id(0) % 2) == 1
    first_step = pl.program_id(1) == 0

    lane = lax.broadcasted_iota(jnp.int32, (span + blk, LANES), 1)
    lo = lane < B_HEAD_DIM

    def strip(prev_ref, own_ref):
        x = jnp.concatenate([prev_ref[...], own_ref[...]], axis=0)
        swapped = pltpu.roll(x, B_HEAD_DIM, axis=1)
        mine_lo = jnp.where(odd, swapped, x)
        mine_hi = jnp.where(odd, x, swapped)
        return mine_lo, mine_hi

    k_lo, k_hi = strip(kp_ref, ko_ref)
    v_lo, v_hi = strip(vp_ref, vo_ref)
    zero = jnp.zeros_like(k_lo)
    k_both = jnp.where(lo, k_lo, k_hi)
    v_first = jnp.where(lo, v_lo, zero)
    v_second = jnp.where(lo, zero, v_hi)

    qlane = lax.broadcasted_iota(jnp.int32, (rows, LANES), 1)
    q_is_first = qlane < B_HEAD_DIM

    chains = []
    for u in range(nq):
        band = slice(u * blk, (u + 2) * blk)
        q = q_ref[:, u * blk:(u + 1) * blk, :].reshape(rows, LANES)
        qzero = jnp.zeros_like(q)
        bias = bias_ref[1] if u else bias_ref[jnp.where(first_step, 0, 1)]
        for q_half, v_half, sink in (
                (jnp.where(q_is_first, q, qzero), v_first, sink_ref[0:1, :]),
                (jnp.where(q_is_first, qzero, q), v_second, sink_ref[1:2, :])):
            s = lax.dot_general(k_both[band], q_half, _NT, preferred_element_type=F32)
            chains.append((s + bias, v_half[band], sink))
    probs = []
    for s, v_band, sink in chains:
        m = jnp.maximum(jnp.max(s, axis=0, keepdims=True), sink)
        p = jnp.exp2(s - m)
        denom = jnp.sum(p, axis=0, keepdims=True) + jnp.exp2(sink - m)
        probs.append((p.astype(BF16), v_band, denom))
    outs = [lax.dot_general(v_band, p, _TN, preferred_element_type=F32) / denom
            for p, v_band, denom in probs]
    for u in range(nq):
        out_t = outs[2 * u] + outs[2 * u + 1]
        for pr in range(pairs):
            o_ref[u * blk:(u + 1) * blk, pr * LANES:(pr + 1) * LANES] = (
                out_t[:, pr * blk:(pr + 1) * blk].T.astype(o_ref.dtype))


def _swa_attention(q, k, v, sinks):
    npair, s, _ = q.shape
    hkv = k.shape[1] // B_HEAD_DIM
    pairs = npair // hkv
    blk = SWA_BLOCK
    nq = SWA_STEP_BLOCKS
    span = nq * blk
    rows = pairs * blk
    sink_rows = jnp.repeat(
        (sinks.astype(F32) * LOG2_E).reshape(hkv, pairs, 2).transpose(0, 2, 1), blk, axis=-1)
    kk = np.arange(2 * blk)[:, None]
    tt = np.arange(rows)[None, :] % blk
    in_window = (kk > tt) & (kk <= tt + blk)
    band_bias = jnp.asarray(
        np.where(np.stack([in_window & (kk >= blk), in_window]), 0.0, NEG), dtype=F32)
    prev = pl.BlockSpec((blk, LANES), lambda hh, n: (jnp.maximum(n * nq - 1, 0), hh // 2))
    own = pl.BlockSpec((span, LANES), lambda hh, n: (n, hh // 2))
    return pl.pallas_call(
        functools.partial(_swa_kernel, nq=nq),
        grid=(hkv, s // span),
        in_specs=[
            pl.BlockSpec((pairs, span, LANES), lambda hh, n: (hh, n, 0)),
            prev, own, prev, own,
            pl.BlockSpec((None, 2, rows), lambda hh, n: (hh, 0, 0)),
            pl.BlockSpec((2, 2 * blk, rows), lambda hh, n: (0, 0, 0)),
        ],
        out_specs=pl.BlockSpec((span, pairs * LANES), lambda hh, n: (n, hh)),
        out_shape=jax.ShapeDtypeStruct((s, npair * LANES), BF16),
        compiler_params=_params(
            ("parallel", "parallel"),
            2 * pairs * span * LANES * 2 + 2 * (span + blk) * LANES * 2 + 2 * rows * 4
            + 4 * blk * rows * 4,
            12 * blk * rows * 4),
        name="swa_attention",
    )(q, k, k, v, v, sink_rows, band_bias)


def kernel(x, ffn_norm, ffn_w_in, ffn_w_out, attn_norm, a_w_qkv, a_w_o, kv_norm,
           w_kv_shared, b_w_q, b_sinks, b_w_o, final_norm):
    b, s, d = x.shape
    depth = ffn_norm.shape[0]
    n_a = a_w_qkv.shape[0]
    a_q = a_w_o.shape[1]
    a_kv = (a_w_qkv.shape[2] - a_q) // 2
    b_kv = w_kv_shared.shape[1] // 2
    rope_a = _rope_tables(s, A_HEAD_DIM)
    rope_b = _rope_tables(s, B_HEAD_DIM)
    w_in, w_out = ffn_w_in, ffn_w_out.astype(BF16)
    w_qkv, w_ao = a_w_qkv.astype(BF16), a_w_o.astype(BF16)
    w_kv, w_bq, w_bo = w_kv_shared.astype(BF16), b_w_q.astype(BF16), b_w_o.astype(BF16)

    def gains_entering(layer):
        if layer == depth:
            return []
        return ([kv_norm] if layer == n_a else []) + [ffn_norm[layer, 0]]

    outs = []
    for bi in range(b):
        xs = x.reshape(s, d) if b == 1 else x[bi]
        sumsq, hs = _prenorm(xs, jnp.stack(gains_entering(0)))
        k_sh = v_sh = None
        for layer in range(depth):
            hs = list(hs)
            if layer == n_a:
                h_kv = hs.pop(0)
                k_sh = _proj(h_kv, sumsq, w_kv, 0, b_kv, rope=rope_b, half=B_HEAD_DIM // 2)
                v_sh = _proj(h_kv, sumsq, w_kv, b_kv, b_kv)
            act = _swiglu_up(hs[0], sumsq, w_in, (layer, 0))
            xs, sumsq, (h,) = _matmul_residual(act, w_out, (layer, 0), xs, 0.5, 512, 512,
                                               [attn_norm[layer]])
            if layer < n_a:
                w = w_qkv[layer]
                q = _proj(h, sumsq, w, 0, a_q, rope=rope_a, half=A_HEAD_DIM // 2,
                          scale=A_HEAD_DIM ** -0.5 * LOG2_E, group_major=True)
                k = _proj(h, sumsq, w, a_q, a_kv, rope=rope_a, half=A_HEAD_DIM // 2,
                          group_major=True)
                v = _proj(h, sumsq, w, a_q + a_kv, a_kv, group_major=True)
                o = _moba_attention(q, k, v)
                w_o, lead = w_ao, (layer,)
            else:
                li = layer - n_a
                q = _proj(h, sumsq, w_bq[li], 0, w_bq.shape[2], rope=rope_b,
                          half=B_HEAD_DIM // 2, scale=B_HEAD_DIM ** -0.5 * LOG2_E, group_major=True)
                o = _swa_attention(q, k_sh, v_sh, b_sinks[li])
                w_o, lead = w_bo, (li,)
            xs, sumsq, (h,) = _matmul_residual(o, w_o, lead, xs, 1.0, 1024, 512,
                                               [ffn_norm[layer, 1]])
            act = _swiglu_up(h, sumsq, w_in, (layer, 1))
            xs, sumsq, hs = _matmul_residual(act, w_out, (layer, 1), xs, 0.5, 512, 512,
                                             gains_entering(layer + 1))
        outs.append(_final_norm(xs, sumsq, final_norm))
    return outs[0].reshape(b, s, d) if b == 1 else jnp.stack(outs, axis=0)
```

```python
import functools

import jax
import jax.numpy as jnp
import numpy as np
from jax import lax
from jax.experimental import pallas as pl
from jax.experimental.pallas import tpu as pltpu

F32 = jnp.float32
BF16 = jnp.bfloat16

RMS_EPS = 1e-6
ROPE_THETA = 10000.0
NEG = -1e30
LOG2_E = 1.4426950408889634
TAKEN = -3e38

A_HEAD_DIM = 128
A_GROUP = 4
MOBA_BLOCK = 256
MOBA_TOPK = 3

B_HEAD_DIM = 64
B_GROUP = 8
SWA_BLOCK = 128
SWA_STEP_BLOCKS = 8

LANES = 128
MXU_COLS = 256
BF16_SUBLANES = 16
UP_TILE_N = 256
V7X_VMEM_CAP = 60000 * 1024

_NT = (((1,), (1,)), ((), ()))
_TN = (((0,), (0,)), ((), ()))


def _params(semantics, block_bytes, scratch_bytes=0):
    need = 2 * block_bytes + scratch_bytes
    return pltpu.CompilerParams(
        dimension_semantics=semantics,
        vmem_limit_bytes=int(min(V7X_VMEM_CAP, max(need, 16 * 1024 * 1024))),
    )


def _lane_tile_sum(x):
    total = x[:, 0:LANES]
    for t in range(1, x.shape[1] // LANES):
        total = total + x[:, t * LANES:(t + 1) * LANES]
    return total


def _row_factor(sumsq_ref, width):
    mean = jnp.sum(sumsq_ref[...], axis=-1, keepdims=True) * (1.0 / width)
    return jnp.broadcast_to(lax.rsqrt(mean + RMS_EPS), sumsq_ref.shape)


def _prenorm_kernel(x_ref, g_ref, sumsq_ref, *o_refs):
    x = x_ref[...]
    sumsq_ref[...] = _lane_tile_sum(x * x)
    for n, o_ref in enumerate(o_refs):
        o_ref[...] = (x * g_ref[n:n + 1, :]).astype(o_ref.dtype)


def _prenorm(x, gains):
    s, d = x.shape
    n = gains.shape[0]
    tm = min(512, s)
    row = pl.BlockSpec((tm, d), lambda i: (i, 0))
    outs = pl.pallas_call(
        _prenorm_kernel,
        grid=(s // tm,),
        in_specs=[row, pl.BlockSpec((n, d), lambda i: (0, 0))],
        out_specs=[pl.BlockSpec((tm, LANES), lambda i: (i, 0))] + [row] * n,
        out_shape=[jax.ShapeDtypeStruct((s, LANES), F32)] + [jax.ShapeDtypeStruct((s, d), BF16)] * n,
        compiler_params=_params(("parallel",), tm * d * (4 + 2 * n), tm * d * 4),
        name="prenorm",
    )(x, gains)
    return outs[0], outs[1:]


def _final_norm_kernel(x_ref, sumsq_ref, g_ref, o_ref):
    mean = jnp.sum(sumsq_ref[...], axis=-1, keepdims=True) * (1.0 / x_ref.shape[1])
    o_ref[...] = x_ref[...] * lax.rsqrt(mean + RMS_EPS) * g_ref[...]


def _final_norm(x, sumsq, gain):
    s, d = x.shape
    tm = min(512, s)
    row = pl.BlockSpec((tm, d), lambda i: (i, 0))
    return pl.pallas_call(
        _final_norm_kernel,
        grid=(s // tm,),
        in_specs=[row, pl.BlockSpec((tm, LANES), lambda i: (i, 0)),
                  pl.BlockSpec((1, d), lambda i: (0, 0))],
        out_specs=row,
        out_shape=jax.ShapeDtypeStruct((s, d), F32),
        compiler_params=_params(("parallel",), tm * d * 8, tm * d * 4),
        name="final_norm",
    )(x, sumsq, gain[None, :])


def _swiglu_up_kernel(h_ref, sumsq_ref, wg_ref, wu_ref, o_ref, r_ref):
    @pl.when(pl.program_id(1) == 0)
    def _row_factors():
        r_ref[...] = _row_factor(sumsq_ref, h_ref.shape[1])

    h = h_ref[...]
    r = r_ref[...]
    g = jnp.dot(h, wg_ref[...].astype(BF16), preferred_element_type=F32)
    u = jnp.dot(h, wu_ref[...].astype(BF16), preferred_element_type=F32)
    for t in range(g.shape[1] // LANES):
        lanes = slice(t * LANES, (t + 1) * LANES)
        gt = g[:, lanes] * r
        o_ref[:, lanes] = (gt * (1.0 / (1.0 + jnp.exp(-gt))) * (u[:, lanes] * r)).astype(o_ref.dtype)


def _swiglu_up(h, sumsq, w_in, lead):
    s, d = h.shape
    f = w_in.shape[-1] // 2
    tm = min(1024, s)
    tn = UP_TILE_N
    nj = f // tn
    squeezed = (None,) * len(lead)
    return pl.pallas_call(
        _swiglu_up_kernel,
        grid=(s // tm, nj),
        in_specs=[
            pl.BlockSpec((tm, d), lambda i, j: (i, 0)),
            pl.BlockSpec((tm, LANES), lambda i, j: (i, 0)),
            pl.BlockSpec(squeezed + (d, tn), lambda i, j: lead + (0, j)),
            pl.BlockSpec(squeezed + (d, tn), lambda i, j: lead + (0, j + nj)),
        ],
        out_specs=pl.BlockSpec((tm, tn), lambda i, j: (i, j)),
        out_shape=jax.ShapeDtypeStruct((s, f), BF16),
        scratch_shapes=[pltpu.VMEM((tm, LANES), F32)],
        compiler_params=_params(
            ("parallel", "arbitrary"),
            tm * d * 2 + 2 * d * tn * w_in.dtype.itemsize + tm * tn * 2,
            4 * tm * tn * 4 + 2 * d * tn * 2),
        name="swiglu_up",
    )(h, sumsq, w_in, w_in)


def _matmul_residual_kernel(a_ref, w_ref, r_ref, *refs, alpha, n_gains):
    if n_gains:
        g_ref, o_ref, sumsq_ref, *h_refs = refs
    else:
        o_ref, sumsq_ref = refs
        h_refs = []
    acc = jnp.dot(a_ref[...], w_ref[...].astype(BF16), preferred_element_type=F32)
    x_new = r_ref[...] + alpha * acc
    o_ref[...] = x_new
    part = _lane_tile_sum(x_new * x_new)

    @pl.when(pl.program_id(1) == 0)
    def _first():
        sumsq_ref[...] = part

    @pl.when(pl.program_id(1) != 0)
    def _rest():
        sumsq_ref[...] += part

    for n, h_ref in enumerate(h_refs):
        h_ref[...] = (x_new * g_ref[n:n + 1, :]).astype(h_ref.dtype)


def _matmul_residual(a, w, lead, res, alpha, tm, tn, next_gains):
    s, k = a.shape
    n = w.shape[-1]
    ng = len(next_gains)
    tm, tn = min(tm, s), min(tn, n)
    squeezed = (None,) * len(lead)
    tile = pl.BlockSpec((tm, tn), lambda i, j: (i, j))
    in_specs = [
        pl.BlockSpec((tm, k), lambda i, j: (i, 0)),
        pl.BlockSpec(squeezed + (k, tn), lambda i, j: lead + (0, j)),
        tile,
    ]
    args = [a, w, res]
    if ng:
        in_specs.append(pl.BlockSpec((ng, tn), lambda i, j: (0, j)))
        args.append(jnp.stack(next_gains))
    outs = pl.pallas_call(
        functools.partial(_matmul_residual_kernel, alpha=alpha, n_gains=ng),
        grid=(s // tm, n // tn),
        in_specs=in_specs,
        out_specs=[tile, pl.BlockSpec((tm, LANES), lambda i, j: (i, 0))] + [tile] * ng,
        out_shape=[jax.ShapeDtypeStruct((s, n), F32), jax.ShapeDtypeStruct((s, LANES), F32)]
        + [jax.ShapeDtypeStruct((s, n), BF16)] * ng,
        compiler_params=_params(
            ("parallel", "arbitrary"),
            tm * k * 2 + k * tn * w.dtype.itemsize + 2 * tm * tn * 4 + tm * LANES * 4
            + ng * tm * tn * 2,
            3 * tm * tn * 4 + k * tn * 2),
        name="matmul_residual",
    )(*args)
    return outs[0], outs[1], outs[2:]


def _rope_tables(seq, dim):
    half = dim // 2
    inv = 1.0 / (ROPE_THETA ** (jnp.arange(0, dim, 2, dtype=F32) / dim))
    ang = jnp.arange(seq, dtype=F32)[:, None] * inv[None, :]
    cos, sin = jnp.cos(ang), jnp.sin(ang)
    zero = jnp.zeros_like(sin)
    reps = LANES // dim
    cos_t = jnp.tile(jnp.concatenate([cos, cos], axis=-1), (1, reps))
    s_lo = jnp.tile(jnp.concatenate([-sin, zero], axis=-1), (1, reps))
    s_hi = jnp.tile(jnp.concatenate([zero, sin], axis=-1), (1, reps))
    return cos_t, s_lo, s_hi


def _proj_kernel(h_ref, sumsq_ref, w_ref, *refs, half, scale, group_major):
    acc = jnp.dot(h_ref[...], w_ref[...].astype(BF16), preferred_element_type=F32)
    if half:
        cos_ref, slo_ref, shi_ref, o_ref = refs
        cos, s_lo, s_hi = cos_ref[...], slo_ref[...], shi_ref[...]
        one_roll = 2 * half == LANES
        if one_roll:
            s_both = s_lo + s_hi
    else:
        (o_ref,) = refs
    r = _row_factor(sumsq_ref, h_ref.shape[1]) * scale
    for g in range(acc.shape[1] // LANES):
        x = acc[:, g * LANES:(g + 1) * LANES] * r
        if half and one_roll:
            x = x * cos + pltpu.roll(x, half, axis=1) * s_both
        elif half:
            x = (x * cos + pltpu.roll(x, LANES - half, axis=1) * s_lo
                 + pltpu.roll(x, half, axis=1) * s_hi)
        if group_major:
            o_ref[g] = x.astype(o_ref.dtype)
        else:
            o_ref[:, g * LANES:(g + 1) * LANES] = x.astype(o_ref.dtype)


def _proj(h, sumsq, w, col0, ncols, *, rope=None, half=0, scale=1.0, group_major=False):
    s, k = h.shape
    tm = min(1024, s)
    tn = min(512, ncols)
    j0 = col0 // tn
    in_specs = [
        pl.BlockSpec((tm, k), lambda i, j: (i, 0)),
        pl.BlockSpec((tm, LANES), lambda i, j: (i, 0)),
        pl.BlockSpec((k, tn), lambda i, j: (0, j + j0)),
    ]
    args = [h, sumsq, w]
    if half:
        tab = pl.BlockSpec((tm, LANES), lambda i, j: (i, 0))
        in_specs += [tab, tab, tab]
        args += list(rope)
    if group_major:
        gpt = tn // LANES
        out_spec = pl.BlockSpec((gpt, tm, LANES), lambda i, j: (j, i, 0))
        out_shape = jax.ShapeDtypeStruct((ncols // LANES, s, LANES), BF16)
    else:
        out_spec = pl.BlockSpec((tm, tn), lambda i, j: (i, j))
        out_shape = jax.ShapeDtypeStruct((s, ncols), BF16)
    return pl.pallas_call(
        functools.partial(_proj_kernel, half=half, scale=scale, group_major=group_major),
        grid=(s // tm, ncols // tn),
        in_specs=in_specs,
        out_specs=out_spec,
        out_shape=out_shape,
        compiler_params=_params(
            ("parallel", "arbitrary"),
            tm * k * 2 + k * tn * w.dtype.itemsize + 4 * tm * LANES * 4 + tm * tn * 2,
            3 * tm * tn * 4 + k * tn * 2),
        name="proj",
    )(*args)


def _moba_kernel(q_ref, k_ref, vt_ref, o_ref, kmean_ref, qa_ref, sa_ref, sb_ref, m_ref, acc_ref):
    group, blk, dh = q_ref.shape
    nb = vt_ref.shape[0]
    rows = group * blk
    i = pl.program_id(1)

    @pl.when(i == 0)
    def _block_means():
        def body(j, carry):
            kj = k_ref[pl.ds(pl.multiple_of(j * blk, blk), blk), 0:dh].astype(F32)
            kmean_ref[pl.ds(j, 1), :] = jnp.sum(kj, axis=0, keepdims=True) * (1.0 / blk)
            return carry
        lax.fori_loop(0, nb, body, 0)

    q_t = q_ref[...].reshape(rows, dh).astype(F32).T.astype(BF16)
    qa_ref[0:dh, :] = q_t

    own = pl.multiple_of(i * blk, blk)
    s_own = jnp.dot(k_ref[pl.ds(own, blk), 0:dh], q_t, preferred_element_type=F32)

    km = kmean_ref[...]
    km_hi = km.astype(BF16)
    km_lo = (km - km_hi.astype(F32)).astype(BF16)
    gate = (jnp.dot(km_hi, q_t, preferred_element_type=F32)
            + jnp.dot(km_lo, q_t, preferred_element_type=F32))
    bidx = lax.broadcasted_iota(jnp.int32, (nb, rows), 0)
    past = bidx < i
    bidx = bidx.astype(F32)
    gate = jnp.where(past, gate, NEG)
    bias = jnp.full((nb, rows), NEG, F32)
    for _ in range(min(MOBA_TOPK, nb)):
        top = jnp.max(gate, axis=0, keepdims=True)
        first = jnp.min(jnp.where(gate == top, bidx, float(nb)), axis=0, keepdims=True)
        pick = bidx == first
        bias = jnp.where(pick & past, 0.0, bias)
        gate = jnp.where(pick, TAKEN, gate)
    qa_ref[dh:dh + nb, :] = bias.astype(BF16)
    if nb < LANES:
        qa_ref[dh + nb:dh + LANES, :] = jnp.zeros((LANES - nb, rows), BF16)

    s = s_own
    kpos = lax.broadcasted_iota(jnp.int32, (blk, rows), 0)
    qpos = lax.broadcasted_iota(jnp.int32, (blk, rows), 1) & (blk - 1)
    s = jnp.where(kpos <= qpos, s, NEG)
    m = jnp.max(s, axis=0, keepdims=True)
    m_ref[...] = m
    acc_ref[...] = jnp.dot(vt_ref[i], jnp.exp2(s - m).astype(BF16), preferred_element_type=F32)

    def scores(j, s_ref):
        off = pl.multiple_of(jnp.minimum(j, nb - 1) * blk, blk)
        s_ref[...] = jnp.dot(k_ref[pl.ds(off, blk), :], qa_ref[...], preferred_element_type=F32)

    reread = pl.multiple_of(jnp.minimum(i, 0) * blk, blk)

    def update(s_ref, j):
        v_t = vt_ref[jnp.minimum(j, nb - 1)]
        for c in range(rows // MXU_COLS):
            cols = slice(c * MXU_COLS, (c + 1) * MXU_COLS)
            s = s_ref[pl.ds(reread, blk), cols]
            m_old = m_ref[:, cols]
            m_new = jnp.maximum(m_old, jnp.max(s, axis=0, keepdims=True))
            m_ref[:, cols] = m_new
            p = jnp.exp2(s - m_new).astype(BF16)
            acc_ref[:, cols] = jnp.exp2(m_old - m_new) * acc_ref[:, cols] + jnp.dot(
                v_t, p, preferred_element_type=F32)

    scores(0, sa_ref)

    def pair(j):
        scores(j + 1, sb_ref)
        update(sa_ref, j)
        scores(j + 2, sa_ref)
        update(sb_ref, j + 1)

    def quad_trip(t, carry):
        pair(4 * t)
        pair(4 * t + 2)
        return carry
    n_quads = i // 4
    lax.fori_loop(0, n_quads, quad_trip, 0)

    def pair_trip(t, carry):
        pair(4 * n_quads + 2 * t)
        return carry
    lax.fori_loop(0, (i - 4 * n_quads + 1) // 2, pair_trip, 0)

    out_t = acc_ref[0:dh, :] / acc_ref[dh:dh + 1, :]
    for g in range(group):
        o_ref[:, g * dh:(g + 1) * dh] = out_t[:, g * blk:(g + 1) * blk].T.astype(o_ref.dtype)


def _moba_attention(q, k, v):
    h, s, dh = q.shape
    hkv = k.shape[0]
    group = h // hkv
    blk = MOBA_BLOCK
    nb = s // blk
    rows = group * blk
    assert nb <= LANES and dh == LANES
    block_of_key = jnp.arange(s, dtype=jnp.int32)[:, None] // blk
    onehot = (block_of_key == jnp.arange(LANES, dtype=jnp.int32)[None, :]).astype(BF16)
    k_aug = jnp.concatenate([k, jnp.broadcast_to(onehot, (hkv, s, LANES))], axis=-1)
    v_t = v.reshape(hkv, nb, blk, dh).transpose(0, 1, 3, 2)
    ones_rows = jnp.zeros((hkv, nb, BF16_SUBLANES, blk), BF16).at[:, :, 0, :].set(1.0)
    vt_aug = jnp.concatenate([v_t, ones_rows], axis=2)
    vrows = dh + BF16_SUBLANES
    return pl.pallas_call(
        _moba_kernel,
        grid=(hkv, nb),
        in_specs=[
            pl.BlockSpec((group, blk, dh), lambda hh, i: (hh, i, 0)),
            pl.BlockSpec((None, s, 2 * LANES), lambda hh, i: (hh, 0, 0)),
            pl.BlockSpec((None, nb, vrows, blk), lambda hh, i: (hh, 0, 0, 0)),
        ],
        out_specs=pl.BlockSpec((blk, group * dh), lambda hh, i: (i, hh)),
        out_shape=jax.ShapeDtypeStruct((s, h * dh), BF16),
        scratch_shapes=[
            pltpu.VMEM((nb, dh), F32),
            pltpu.VMEM((2 * LANES, rows), BF16),
            pltpu.VMEM((blk, rows), F32),
            pltpu.VMEM((blk, rows), F32),
            pltpu.VMEM((1, rows), F32),
            pltpu.VMEM((vrows, rows), F32),
        ],
        compiler_params=_params(
            ("arbitrary", "arbitrary"),
            rows * dh * 2 + s * 2 * LANES * 2 + nb * vrows * blk * 2 + blk * group * dh * 2,
            (2 * blk + vrows + LANES + 1) * rows * 4 + 6 * blk * rows * 4),
        name="moba_attention",
    )(q, k_aug, vt_aug)


def _swa_kernel(q_ref, kp_ref, ko_ref, vp_ref, vo_ref, sink_ref, bias_ref, o_ref, *, nq):
    pairs, span, _ = q_ref.shape
    blk = span // nq
    rows = pairs * blk
    odd = (pl.program_id(0) % 2) == 1
    first_step = pl.program_id(1) == 0

    lane = lax.broadcasted_iota(jnp.int32, (span + blk, LANES), 1)
    lo = lane < B_HEAD_DIM

    def strip(prev_ref, own_ref):
        x = jnp.concatenate([prev_ref[...], own_ref[...]], axis=0)
        swapped = pltpu.roll(x, B_HEAD_DIM, axis=1)
        mine_lo = jnp.where(odd, swapped, x)
        mine_hi = jnp.where(odd, x, swapped)
        return mine_lo, mine_hi

    k_lo, k_hi = strip(kp_ref, ko_ref)
    v_lo, v_hi = strip(vp_ref, vo_ref)
    zero = jnp.zeros_like(k_lo)
    k_both = jnp.where(lo, k_lo, k_hi)
    v_first = jnp.where(lo, v_lo, zero)
    v_second = jnp.where(lo, zero, v_hi)

    qlane = lax.broadcasted_iota(jnp.int32, (rows, LANES), 1)
    q_is_first = qlane < B_HEAD_DIM

    chains = []
    for u in range(nq):
        band = slice(u * blk, (u + 2) * blk)
        q = q_ref[:, u * blk:(u + 1) * blk, :].reshape(rows, LANES)
        qzero = jnp.zeros_like(q)
        bias = bias_ref[1] if u else bias_ref[jnp.where(first_step, 0, 1)]
        for q_half, v_half, sink in (
                (jnp.where(q_is_first, q, qzero), v_first, sink_ref[0:1, :]),
                (jnp.where(q_is_first, qzero, q), v_second, sink_ref[1:2, :])):
            s = lax.dot_general(k_both[band], q_half, _NT, preferred_element_type=F32)
            chains.append((s + bias, v_half[band], sink))
    probs = []
    for s, v_band, sink in chains:
        m = jnp.maximum(jnp.max(s, axis=0, keepdims=True), sink)
        p = jnp.exp2(s - m)
        denom = jnp.sum(p, axis=0, keepdims=True) + jnp.exp2(sink - m)
        probs.append((p.astype(BF16), v_band, denom))
    outs = [lax.dot_general(v_band, p, _TN, preferred_element_type=F32) / denom
            for p, v_band, denom in probs]
    for u in range(nq):
        out_t = outs[2 * u] + outs[2 * u + 1]
        for pr in range(pairs):
            o_ref[u * blk:(u + 1) * blk, pr * LANES:(pr + 1) * LANES] = (
                out_t[:, pr * blk:(pr + 1) * blk].T.astype(o_ref.dtype))


def _swa_attention(q, k, v, sinks):
    npair, s, _ = q.shape
    hkv = k.shape[1] // B_HEAD_DIM
    pairs = npair // hkv
    blk = SWA_BLOCK
    nq = SWA_STEP_BLOCKS
    span = nq * blk
    rows = pairs * blk
    sink_rows = jnp.repeat(
        (sinks.astype(F32) * LOG2_E).reshape(hkv, pairs, 2).transpose(0, 2, 1), blk, axis=-1)
    kk = np.arange(2 * blk)[:, None]
    tt = np.arange(rows)[None, :] % blk
    in_window = (kk > tt) & (kk <= tt + blk)
    band_bias = jnp.asarray(
        np.where(np.stack([in_window & (kk >= blk), in_window]), 0.0, NEG), dtype=F32)
    prev = pl.BlockSpec((blk, LANES), lambda hh, n: (jnp.maximum(n * nq - 1, 0), hh // 2))
    own = pl.BlockSpec((span, LANES), lambda hh, n: (n, hh // 2))
    return pl.pallas_call(
        functools.partial(_swa_kernel, nq=nq),
        grid=(hkv, s // span),
        in_specs=[
            pl.BlockSpec((pairs, span, LANES), lambda hh, n: (hh, n, 0)),
            prev, own, prev, own,
            pl.BlockSpec((None, 2, rows), lambda hh, n: (hh, 0, 0)),
            pl.BlockSpec((2, 2 * blk, rows), lambda hh, n: (0, 0, 0)),
        ],
        out_specs=pl.BlockSpec((span, pairs * LANES), lambda hh, n: (n, hh)),
        out_shape=jax.ShapeDtypeStruct((s, npair * LANES), BF16),
        compiler_params=_params(
            ("parallel", "parallel"),
            2 * pairs * span * LANES * 2 + 2 * (span + blk) * LANES * 2 + 2 * rows * 4
            + 4 * blk * rows * 4,
            12 * blk * rows * 4),
        name="swa_attention",
    )(q, k, k, v, v, sink_rows, band_bias)


def kernel(x, ffn_norm, ffn_w_in, ffn_w_out, attn_norm, a_w_qkv, a_w_o, kv_norm,
           w_kv_shared, b_w_q, b_sinks, b_w_o, final_norm):
    b, s, d = x.shape
    depth = ffn_norm.shape[0]
    n_a = a_w_qkv.shape[0]
    a_q = a_w_o.shape[1]
    a_kv = (a_w_qkv.shape[2] - a_q) // 2
    b_kv = w_kv_shared.shape[1] // 2
    rope_a = _rope_tables(s, A_HEAD_DIM)
    rope_b = _rope_tables(s, B_HEAD_DIM)
    w_in, w_out = ffn_w_in, ffn_w_out.astype(BF16)
    w_qkv, w_ao = a_w_qkv.astype(BF16), a_w_o.astype(BF16)
    w_kv, w_bq, w_bo = w_kv_shared.astype(BF16), b_w_q.astype(BF16), b_w_o.astype(BF16)

    def gains_entering(layer):
        if layer == depth:
            return []
        return ([kv_norm] if layer == n_a else []) + [ffn_norm[layer, 0]]

    outs = []
    for bi in range(b):
        xs = x.reshape(s, d) if b == 1 else x[bi]
        sumsq, hs = _prenorm(xs, jnp.stack(gains_entering(0)))
        k_sh = v_sh = None
        for layer in range(depth):
            hs = list(hs)
            if layer == n_a:
                h_kv = hs.pop(0)
                k_sh = _proj(h_kv, sumsq, w_kv, 0, b_kv, rope=rope_b, half=B_HEAD_DIM // 2)
                v_sh = _proj(h_kv, sumsq, w_kv, b_kv, b_kv)
            act = _swiglu_up(hs[0], sumsq, w_in, (layer, 0))
            xs, sumsq, (h,) = _matmul_residual(act, w_out, (layer, 0), xs, 0.5, 512, 512,
                                               [attn_norm[layer]])
            if layer < n_a:
                w = w_qkv[layer]
                q = _proj(h, sumsq, w, 0, a_q, rope=rope_a, half=A_HEAD_DIM // 2,
                          scale=A_HEAD_DIM ** -0.5 * LOG2_E, group_major=True)
                k = _proj(h, sumsq, w, a_q, a_kv, rope=rope_a, half=A_HEAD_DIM // 2,
                          group_major=True)
                v = _proj(h, sumsq, w, a_q + a_kv, a_kv, group_major=True)
                o = _moba_attention(q, k, v)
                w_o, lead = w_ao, (layer,)
            else:
                li = layer - n_a
                q = _proj(h, sumsq, w_bq[li], 0, w_bq.shape[2], rope=rope_b,
                          half=B_HEAD_DIM // 2, scale=B_HEAD_DIM ** -0.5 * LOG2_E, group_major=True)
                o = _swa_attention(q, k_sh, v_sh, b_sinks[li])
                w_o, lead = w_bo, (li,)
            xs, sumsq, (h,) = _matmul_residual(o, w_o, lead, xs, 1.0, 1024, 512,
                                               [ffn_norm[layer, 1]])
            act = _swiglu_up(h, sumsq, w_in, (layer, 1))
            xs, sumsq, hs = _matmul_residual(act, w_out, (layer, 1), xs, 0.5, 512, 512,
                                             gains_entering(layer + 1))
        outs.append(_final_norm(xs, sumsq, final_norm))
    return outs[0].reshape(b, s, d) if b == 1 else jnp.stack(outs, axis=0)
```

```python
import functools

import jax
import jax.numpy as jnp
import numpy as np
from jax import lax
from jax.experimental import pallas as pl
from jax.experimental.pallas import tpu as pltpu

F32 = jnp.float32
BF16 = jnp.bfloat16

RMS_EPS = 1e-6
ROPE_THETA = 10000.0
NEG = -1e30
LOG2_E = 1.4426950408889634
TAKEN = -3e38

A_HEAD_DIM = 128
A_GROUP = 4
MOBA_BLOCK = 256
MOBA_TOPK = 3

B_HEAD_DIM = 64
B_GROUP = 8
SWA_BLOCK = 128
SWA_STEP_BLOCKS = 8

LANES = 128
MXU_COLS = 256
BF16_SUBLANES = 16
UP_TILE_N = 256
V7X_VMEM_CAP = 60000 * 1024

_NT = (((1,), (1,)), ((), ()))
_TN = (((0,), (0,)), ((), ()))


def _params(semantics, block_bytes, scratch_bytes=0):
    need = 2 * block_bytes + scratch_bytes
    return pltpu.CompilerParams(
        dimension_semantics=semantics,
        vmem_limit_bytes=int(min(V7X_VMEM_CAP, max(need, 16 * 1024 * 1024))),
    )


def _lane_tile_sum(x):
    total = x[:, 0:LANES]
    for t in range(1, x.shape[1] // LANES):
        total = total + x[:, t * LANES:(t + 1) * LANES]
    return total


def _row_factor(sumsq_ref, width):
    mean = jnp.sum(sumsq_ref[...], axis=-1, keepdims=True) * (1.0 / width)
    return jnp.broadcast_to(lax.rsqrt(mean + RMS_EPS), sumsq_ref.shape)


def _prenorm_kernel(x_ref, g_ref, sumsq_ref, *o_refs):
    x = x_ref[...]
    sumsq_ref[...] = _lane_tile_sum(x * x)
    for n, o_ref in enumerate(o_refs):
        o_ref[...] = (x * g_ref[n:n + 1, :]).astype(o_ref.dtype)


def _prenorm(x, gains):
    s, d = x.shape
    n = gains.shape[0]
    tm = min(512, s)
    row = pl.BlockSpec((tm, d), lambda i: (i, 0))
    outs = pl.pallas_call(
        _prenorm_kernel,
        grid=(s // tm,),
        in_specs=[row, pl.BlockSpec((n, d), lambda i: (0, 0))],
        out_specs=[pl.BlockSpec((tm, LANES), lambda i: (i, 0))] + [row] * n,
        out_shape=[jax.ShapeDtypeStruct((s, LANES), F32)] + [jax.ShapeDtypeStruct((s, d), BF16)] * n,
        compiler_params=_params(("parallel",), tm * d * (4 + 2 * n), tm * d * 4),
        name="prenorm",
    )(x, gains)
    return outs[0], outs[1:]


def _final_norm_kernel(x_ref, sumsq_ref, g_ref, o_ref):
    mean = jnp.sum(sumsq_ref[...], axis=-1, keepdims=True) * (1.0 / x_ref.shape[1])
    o_ref[...] = x_ref[...] * lax.rsqrt(mean + RMS_EPS) * g_ref[...]


def _final_norm(x, sumsq, gain):
    s, d = x.shape
    tm = min(512, s)
    row = pl.BlockSpec((tm, d), lambda i: (i, 0))
    return pl.pallas_call(
        _final_norm_kernel,
        grid=(s // tm,),
        in_specs=[row, pl.BlockSpec((tm, LANES), lambda i: (i, 0)),
                  pl.BlockSpec((1, d), lambda i: (0, 0))],
        out_specs=row,
        out_shape=jax.ShapeDtypeStruct((s, d), F32),
        compiler_params=_params(("parallel",), tm * d * 8, tm * d * 4),
        name="final_norm",
    )(x, sumsq, gain[None, :])


def _swiglu_up_kernel(h_ref, sumsq_ref, wg_ref, wu_ref, wd_ref, o_ref, wd_bf16_ref, r_ref):
    @pl.when(pl.program_id(1) == 0)
    def _row_factors():
        r_ref[...] = _row_factor(sumsq_ref, h_ref.shape[1])

    wd_bf16_ref[...] = wd_ref[...].astype(BF16)

    h = h_ref[...]
    r = r_ref[...]
    g = jnp.dot(h, wg_ref[...].astype(BF16), preferred_element_type=F32)
    u = jnp.dot(h, wu_ref[...].astype(BF16), preferred_element_type=F32)
    for t in range(g.shape[1] // LANES):
        lanes = slice(t * LANES, (t + 1) * LANES)
        gt = g[:, lanes] * r
        o_ref[:, lanes] = (gt * (1.0 / (1.0 + jnp.exp(-gt))) * (u[:, lanes] * r)).astype(o_ref.dtype)


def _swiglu_up(h, sumsq, w_in, w_down, lead):
    s, d = h.shape
    f = w_in.shape[-1] // 2
    tm = min(1024, s)
    tn = UP_TILE_N
    nj = f // tn
    squeezed = (None,) * len(lead)
    n_steps = (s // tm) * nj
    share = f // n_steps
    assert share * n_steps == f and share % BF16_SUBLANES == 0, (f, n_steps)
    return pl.pallas_call(
        _swiglu_up_kernel,
        grid=(s // tm, nj),
        in_specs=[
            pl.BlockSpec((tm, d), lambda i, j: (i, 0)),
            pl.BlockSpec((tm, LANES), lambda i, j: (i, 0)),
            pl.BlockSpec(squeezed + (d, tn), lambda i, j: lead + (0, j)),
            pl.BlockSpec(squeezed + (d, tn), lambda i, j: lead + (0, j + nj)),
            pl.BlockSpec(squeezed + (share, d), lambda i, j: lead + (i * nj + j, 0)),
        ],
        out_specs=[pl.BlockSpec((tm, tn), lambda i, j: (i, j)),
                   pl.BlockSpec((share, d), lambda i, j: (i * nj + j, 0))],
        out_shape=[jax.ShapeDtypeStruct((s, f), BF16), jax.ShapeDtypeStruct((f, d), BF16)],
        scratch_shapes=[pltpu.VMEM((tm, LANES), F32)],
        compiler_params=_params(
            ("parallel", "arbitrary"),
            tm * d * 2 + 2 * d * tn * w_in.dtype.itemsize + tm * tn * 2 + share * d * 6,
            4 * tm * tn * 4 + 2 * d * tn * 2),
        name="swiglu_up",
    )(h, sumsq, w_in, w_in, w_down)


def _matmul_residual_kernel(a_ref, w_ref, r_ref, *refs, alpha, n_gains):
    if n_gains:
        g_ref, o_ref, sumsq_ref, *h_refs = refs
    else:
        o_ref, sumsq_ref = refs
        h_refs = []
    acc = jnp.dot(a_ref[...], w_ref[...].astype(BF16), preferred_element_type=F32)
    x_new = r_ref[...] + alpha * acc
    o_ref[...] = x_new
    part = _lane_tile_sum(x_new * x_new)

    @pl.when(pl.program_id(1) == 0)
    def _first():
        sumsq_ref[...] = part

    @pl.when(pl.program_id(1) != 0)
    def _rest():
        sumsq_ref[...] += part

    for n, h_ref in enumerate(h_refs):
        h_ref[...] = (x_new * g_ref[n:n + 1, :]).astype(h_ref.dtype)


def _matmul_residual(a, w, lead, res, alpha, tm, tn, next_gains):
    s, k = a.shape
    n = w.shape[-1]
    ng = len(next_gains)
    tm, tn = min(tm, s), min(tn, n)
    squeezed = (None,) * len(lead)
    tile = pl.BlockSpec((tm, tn), lambda i, j: (i, j))
    in_specs = [
        pl.BlockSpec((tm, k), lambda i, j: (i, 0)),
        pl.BlockSpec(squeezed + (k, tn), lambda i, j: lead + (0, j)),
        tile,
    ]
    args = [a, w, res]
    if ng:
        in_specs.append(pl.BlockSpec((ng, tn), lambda i, j: (0, j)))
        args.append(jnp.stack(next_gains))
    outs = pl.pallas_call(
        functools.partial(_matmul_residual_kernel, alpha=alpha, n_gains=ng),
        grid=(s // tm, n // tn),
        in_specs=in_specs,
        out_specs=[tile, pl.BlockSpec((tm, LANES), lambda i, j: (i, 0))] + [tile] * ng,
        out_shape=[jax.ShapeDtypeStruct((s, n), F32), jax.ShapeDtypeStruct((s, LANES), F32)]
        + [jax.ShapeDtypeStruct((s, n), BF16)] * ng,
        compiler_params=_params(
            ("parallel", "arbitrary"),
            tm * k * 2 + k * tn * w.dtype.itemsize + 2 * tm * tn * 4 + tm * LANES * 4
            + ng * tm * tn * 2,
            3 * tm * tn * 4 + k * tn * 2),
        name="matmul_residual",
    )(*args)
    return outs[0], outs[1], outs[2:]


def _rope_tables(seq, dim):
    half = dim // 2
    inv = 1.0 / (ROPE_THETA ** (jnp.arange(0, dim, 2, dtype=F32) / dim))
    ang = jnp.arange(seq, dtype=F32)[:, None] * inv[None, :]
    cos, sin = jnp.cos(ang), jnp.sin(ang)
    zero = jnp.zeros_like(sin)
    reps = LANES // dim
    cos_t = jnp.tile(jnp.concatenate([cos, cos], axis=-1), (1, reps))
    s_lo = jnp.tile(jnp.concatenate([-sin, zero], axis=-1), (1, reps))
    s_hi = jnp.tile(jnp.concatenate([zero, sin], axis=-1), (1, reps))
    return cos_t, s_lo, s_hi


def _proj_kernel(h_ref, sumsq_ref, w_ref, *refs, half, scale, group_major):
    acc = jnp.dot(h_ref[...], w_ref[...].astype(BF16), preferred_element_type=F32)
    if half:
        cos_ref, slo_ref, shi_ref, o_ref = refs
        cos, s_lo, s_hi = cos_ref[...], slo_ref[...], shi_ref[...]
        one_roll = 2 * half == LANES
        if one_roll:
            s_both = s_lo + s_hi
    else:
        (o_ref,) = refs
    r = _row_factor(sumsq_ref, h_ref.shape[1]) * scale
    for g in range(acc.shape[1] // LANES):
        x = acc[:, g * LANES:(g + 1) * LANES] * r
        if half and one_roll:
            x = x * cos + pltpu.roll(x, half, axis=1) * s_both
        elif half:
            x = (x * cos + pltpu.roll(x, LANES - half, axis=1) * s_lo
                 + pltpu.roll(x, half, axis=1) * s_hi)
        if group_major:
            o_ref[g] = x.astype(o_ref.dtype)
        else:
            o_ref[:, g * LANES:(g + 1) * LANES] = x.astype(o_ref.dtype)


def _proj(h, sumsq, w, col0, ncols, *, rope=None, half=0, scale=1.0, group_major=False):
    s, k = h.shape
    tm = min(1024, s)
    tn = min(512, ncols)
    j0 = col0 // tn
    in_specs = [
        pl.BlockSpec((tm, k), lambda i, j: (i, 0)),
        pl.BlockSpec((tm, LANES), lambda i, j: (i, 0)),
        pl.BlockSpec((k, tn), lambda i, j: (0, j + j0)),
    ]
    args = [h, sumsq, w]
    if half:
        tab = pl.BlockSpec((tm, LANES), lambda i, j: (i, 0))
        in_specs += [tab, tab, tab]
        args += list(rope)
    if group_major:
        gpt = tn // LANES
        out_spec = pl.BlockSpec((gpt, tm, LANES), lambda i, j: (j, i, 0))
        out_shape = jax.ShapeDtypeStruct((ncols // LANES, s, LANES), BF16)
    else:
        out_spec = pl.BlockSpec((tm, tn), lambda i, j: (i, j))
        out_shape = jax.ShapeDtypeStruct((s, ncols), BF16)
    return pl.pallas_call(
        functools.partial(_proj_kernel, half=half, scale=scale, group_major=group_major),
        grid=(s // tm, ncols // tn),
        in_specs=in_specs,
        out_specs=out_spec,
        out_shape=out_shape,
        compiler_params=_params(
            ("parallel", "arbitrary"),
            tm * k * 2 + k * tn * w.dtype.itemsize + 4 * tm * LANES * 4 + tm * tn * 2,
            3 * tm * tn * 4 + k * tn * 2),
        name="proj",
    )(*args)


def _moba_kernel(q_ref, k_ref, vt_ref, o_ref, kmean_ref, qa_ref, sa_ref, sb_ref, m_ref, acc_ref):
    group, blk, dh = q_ref.shape
    nb = vt_ref.shape[0]
    rows = group * blk
    i = pl.program_id(1)

    @pl.when(i == 0)
    def _block_means():
        def body(j, carry):
            kj = k_ref[pl.ds(pl.multiple_of(j * blk, blk), blk), 0:dh].astype(F32)
            kmean_ref[pl.ds(j, 1), :] = jnp.sum(kj, axis=0, keepdims=True) * (1.0 / blk)
            return carry
        lax.fori_loop(0, nb, body, 0)

    q_t = q_ref[...].reshape(rows, dh).astype(F32).T.astype(BF16)
    qa_ref[0:dh, :] = q_t

    own = pl.multiple_of(i * blk, blk)
    s_own = jnp.dot(k_ref[pl.ds(own, blk), 0:dh], q_t, preferred_element_type=F32)

    km = kmean_ref[...]
    km_hi = km.astype(BF16)
    km_lo = (km - km_hi.astype(F32)).astype(BF16)
    gate = (jnp.dot(km_hi, q_t, preferred_element_type=F32)
            + jnp.dot(km_lo, q_t, preferred_element_type=F32))
    bidx = lax.broadcasted_iota(jnp.int32, (nb, rows), 0)
    past = bidx < i
    bidx = bidx.astype(F32)
    gate = jnp.where(past, gate, NEG)
    bias = jnp.full((nb, rows), NEG, F32)
    for _ in range(min(MOBA_TOPK, nb)):
        top = jnp.max(gate, axis=0, keepdims=True)
        first = jnp.min(jnp.where(gate == top, bidx, float(nb)), axis=0, keepdims=True)
        pick = bidx == first
        bias = jnp.where(pick & past, 0.0, bias)
        gate = jnp.where(pick, TAKEN, gate)
    qa_ref[dh:dh + nb, :] = bias.astype(BF16)
    if nb < LANES:
        qa_ref[dh + nb:dh + LANES, :] = jnp.zeros((LANES - nb, rows), BF16)

    s = s_own
    kpos = lax.broadcasted_iota(jnp.int32, (blk, rows), 0)
    qpos = lax.broadcasted_iota(jnp.int32, (blk, rows), 1) & (blk - 1)
    s = jnp.where(kpos <= qpos, s, NEG)
    m = jnp.max(s, axis=0, keepdims=True)
    m_ref[...] = m
    acc_ref[...] = jnp.dot(vt_ref[i], jnp.exp2(s - m).astype(BF16), preferred_element_type=F32)

    def scores(j, s_ref):
        off = pl.multiple_of(jnp.minimum(j, nb - 1) * blk, blk)
        s_ref[...] = jnp.dot(k_ref[pl.ds(off, blk), :], qa_ref[...], preferred_element_type=F32)

    reread = pl.multiple_of(jnp.minimum(i, 0) * blk, blk)

    def update(s_ref, j):
        v_t = vt_ref[jnp.minimum(j, nb - 1)]
        for c in range(rows // MXU_COLS):
            cols = slice(c * MXU_COLS, (c + 1) * MXU_COLS)
            s = s_ref[pl.ds(reread, blk), cols]
            m_old = m_ref[:, cols]
            m_new = jnp.maximum(m_old, jnp.max(s, axis=0, keepdims=True))
            m_ref[:, cols] = m_new
            p = jnp.exp2(s - m_new).astype(BF16)
            acc_ref[:, cols] = jnp.exp2(m_old - m_new) * acc_ref[:, cols] + jnp.dot(
                v_t, p, preferred_element_type=F32)

    scores(0, sa_ref)

    def pair(j):
        scores(j + 1, sb_ref)
        update(sa_ref, j)
        scores(j + 2, sa_ref)
        update(sb_ref, j + 1)

    def quad_trip(t, carry):
        pair(4 * t)
        pair(4 * t + 2)
        return carry
    n_quads = i // 4
    lax.fori_loop(0, n_quads, quad_trip, 0)

    def pair_trip(t, carry):
        pair(4 * n_quads + 2 * t)
        return carry
    lax.fori_loop(0, (i - 4 * n_quads + 1) // 2, pair_trip, 0)

    out_t = acc_ref[0:dh, :] / acc_ref[dh:dh + 1, :]
    for g in range(group):
        o_ref[:, g * dh:(g + 1) * dh] = out_t[:, g * blk:(g + 1) * blk].T.astype(o_ref.dtype)


def _moba_attention(q, k, v):
    h, s, dh = q.shape
    hkv = k.shape[0]
    group = h // hkv
    blk = MOBA_BLOCK
    nb = s // blk
    rows = group * blk
    assert nb <= LANES and dh == LANES
    block_of_key = jnp.arange(s, dtype=jnp.int32)[:, None] // blk
    onehot = (block_of_key == jnp.arange(LANES, dtype=jnp.int32)[None, :]).astype(BF16)
    k_aug = jnp.concatenate([k, jnp.broadcast_to(onehot, (hkv, s, LANES))], axis=-1)
    v_t = v.reshape(hkv, nb, blk, dh).transpose(0, 1, 3, 2)
    ones_rows = jnp.zeros((hkv, nb, BF16_SUBLANES, blk), BF16).at[:, :, 0, :].set(1.0)
    vt_aug = jnp.concatenate([v_t, ones_rows], axis=2)
    vrows = dh + BF16_SUBLANES
    return pl.pallas_call(
        _moba_kernel,
        grid=(hkv, nb),
        in_specs=[
            pl.BlockSpec((group, blk, dh), lambda hh, i: (hh, i, 0)),
            pl.BlockSpec((None, s, 2 * LANES), lambda hh, i: (hh, 0, 0)),
            pl.BlockSpec((None, nb, vrows, blk), lambda hh, i: (hh, 0, 0, 0)),
        ],
        out_specs=pl.BlockSpec((blk, group * dh), lambda hh, i: (i, hh)),
        out_shape=jax.ShapeDtypeStruct((s, h * dh), BF16),
        scratch_shapes=[
            pltpu.VMEM((nb, dh), F32),
            pltpu.VMEM((2 * LANES, rows), BF16),
            pltpu.VMEM((blk, rows), F32),
            pltpu.VMEM((blk, rows), F32),
            pltpu.VMEM((1, rows), F32),
            pltpu.VMEM((vrows, rows), F32),
        ],
        compiler_params=_params(
            ("arbitrary", "arbitrary"),
            rows * dh * 2 + s * 2 * LANES * 2 + nb * vrows * blk * 2 + blk * group * dh * 2,
            (2 * blk + vrows + LANES + 1) * rows * 4 + 6 * blk * rows * 4),
        name="moba_attention",
    )(q, k_aug, vt_aug)


def _swa_kernel(q_ref, kp_ref, ko_ref, vp_ref, vo_ref, sink_ref, bias_ref, o_ref, *, nq):
    pairs, span, _ = q_ref.shape
    blk = span // nq
    rows = pairs * blk
    odd = (pl.program_id(0) % 2) == 1
    first_step = pl.program_id(1) == 0

    lane = lax.broadcasted_iota(jnp.int32, (span + blk, LANES), 1)
    lo = lane < B_HEAD_DIM

    def strip(prev_ref, own_ref):
        x = jnp.concatenate([prev_ref[...], own_ref[...]], axis=0)
        swapped = pltpu.roll(x, B_HEAD_DIM, axis=1)
        mine_lo = jnp.where(odd, swapped, x)
        mine_hi = jnp.where(odd, x, swapped)
        return mine_lo, mine_hi

    k_lo, k_hi = strip(kp_ref, ko_ref)
    v_lo, v_hi = strip(vp_ref, vo_ref)
    zero = jnp.zeros_like(k_lo)
    k_both = jnp.where(lo, k_lo, k_hi)
    v_first = jnp.where(lo, v_lo, zero)
    v_second = jnp.where(lo, zero, v_hi)

    qlane = lax.broadcasted_iota(jnp.int32, (rows, LANES), 1)
    q_is_first = qlane < B_HEAD_DIM

    chains = []
    for u in range(nq):
        band = slice(u * blk, (u + 2) * blk)
        q = q_ref[:, u * blk:(u + 1) * blk, :].reshape(rows, LANES)
        qzero = jnp.zeros_like(q)
        bias = bias_ref[1] if u else bias_ref[jnp.where(first_step, 0, 1)]
        for q_half, v_half, sink in (
                (jnp.where(q_is_first, q, qzero), v_first, sink_ref[0:1, :]),
                (jnp.where(q_is_first, qzero, q), v_second, sink_ref[1:2, :])):
            s = lax.dot_general(k_both[band], q_half, _NT, preferred_element_type=F32)
            chains.append((s + bias, v_half[band], sink))
    probs = []
    for s, v_band, sink in chains:
        m = jnp.maximum(jnp.max(s, axis=0, keepdims=True), sink)
        p = jnp.exp2(s - m)
        denom = jnp.sum(p, axis=0, keepdims=True) + jnp.exp2(sink - m)
        probs.append((p.astype(BF16), v_band, denom))
    outs = [lax.dot_general(v_band, p, _TN, preferred_element_type=F32) / denom
            for p, v_band, denom in probs]
    for u in range(nq):
        out_t = outs[2 * u] + outs[2 * u + 1]
        for pr in range(pairs):
            o_ref[u * blk:(u + 1) * blk, pr * LANES:(pr + 1) * LANES] = (
                out_t[:, pr * blk:(pr + 1) * blk].T.astype(o_ref.dtype))


def _swa_attention(q, k, v, sinks):
    npair, s, _ = q.shape
    hkv = k.shape[1] // B_HEAD_DIM
    pairs = npair // hkv
    blk = SWA_BLOCK
    nq = SWA_STEP_BLOCKS
    span = nq * blk
    rows = pairs * blk
    sink_rows = jnp.repeat(
        (sinks.astype(F32) * LOG2_E).reshape(hkv, pairs, 2).transpose(0, 2, 1), blk, axis=-1)
    kk = np.arange(2 * blk)[:, None]
    tt = np.arange(rows)[None, :] % blk
    in_window = (kk > tt) & (kk <= tt + blk)
    band_bias = jnp.asarray(
        np.where(np.stack([in_window & (kk >= blk), in_window]), 0.0, NEG), dtype=F32)
    prev = pl.BlockSpec((blk, LANES), lambda hh, n: (jnp.maximum(n * nq - 1, 0), hh // 2))
    own = pl.BlockSpec((span, LANES), lambda hh, n: (n, hh // 2))
    return pl.pallas_call(
        functools.partial(_swa_kernel, nq=nq),
        grid=(hkv, s // span),
        in_specs=[
            pl.BlockSpec((pairs, span, LANES), lambda hh, n: (hh, n, 0)),
            prev, own, prev, own,
            pl.BlockSpec((None, 2, rows), lambda hh, n: (hh, 0, 0)),
            pl.BlockSpec((2, 2 * blk, rows), lambda hh, n: (0, 0, 0)),
        ],
        out_specs=pl.BlockSpec((span, pairs * LANES), lambda hh, n: (n, hh)),
        out_shape=jax.ShapeDtypeStruct((s, npair * LANES), BF16),
        compiler_params=_params(
            ("parallel", "parallel"),
            2 * pairs * span * LANES * 2 + 2 * (span + blk) * LANES * 2 + 2 * rows * 4
            + 4 * blk * rows * 4,
            12 * blk * rows * 4),
        name="swa_attention",
    )(q, k, k, v, v, sink_rows, band_bias)


def kernel(x, ffn_norm, ffn_w_in, ffn_w_out, attn_norm, a_w_qkv, a_w_o, kv_norm,
           w_kv_shared, b_w_q, b_sinks, b_w_o, final_norm):
    b, s, d = x.shape
    depth = ffn_norm.shape[0]
    n_a = a_w_qkv.shape[0]
    a_q = a_w_o.shape[1]
    a_kv = (a_w_qkv.shape[2] - a_q) // 2
    b_kv = w_kv_shared.shape[1] // 2
    rope_a = _rope_tables(s, A_HEAD_DIM)
    rope_b = _rope_tables(s, B_HEAD_DIM)
    w_in, w_out = ffn_w_in, ffn_w_out
    w_qkv, w_ao = a_w_qkv.astype(BF16), a_w_o.astype(BF16)
    w_kv, w_bq, w_bo = w_kv_shared.astype(BF16), b_w_q.astype(BF16), b_w_o.astype(BF16)

    def gains_entering(layer):
        if layer == depth:
            return []
        return ([kv_norm] if layer == n_a else []) + [ffn_norm[layer, 0]]

    outs = []
    for bi in range(b):
        xs = x.reshape(s, d) if b == 1 else x[bi]
        sumsq, hs = _prenorm(xs, jnp.stack(gains_entering(0)))
        k_sh = v_sh = None
        for layer in range(depth):
            hs = list(hs)
            if layer == n_a:
                h_kv = hs.pop(0)
                k_sh = _proj(h_kv, sumsq, w_kv, 0, b_kv, rope=rope_b, half=B_HEAD_DIM // 2)
                v_sh = _proj(h_kv, sumsq, w_kv, b_kv, b_kv)
            act, w_down = _swiglu_up(hs[0], sumsq, w_in, w_out, (layer, 0))
            xs, sumsq, (h,) = _matmul_residual(act, w_down, (), xs, 0.5, 512, 512,
                                               [attn_norm[layer]])
            if layer < n_a:
                w = w_qkv[layer]
                q = _proj(h, sumsq, w, 0, a_q, rope=rope_a, half=A_HEAD_DIM // 2,
                          scale=A_HEAD_DIM ** -0.5 * LOG2_E, group_major=True)
                k = _proj(h, sumsq, w, a_q, a_kv, rope=rope_a, half=A_HEAD_DIM // 2,
                          group_major=True)
                v = _proj(h, sumsq, w, a_q + a_kv, a_kv, group_major=True)
                o = _moba_attention(q, k, v)
                w_o, lead = w_ao, (layer,)
            else:
                li = layer - n_a
                q = _proj(h, sumsq, w_bq[li], 0, w_bq.shape[2], rope=rope_b,
                          half=B_HEAD_DIM // 2, scale=B_HEAD_DIM ** -0.5 * LOG2_E, group_major=True)
                o = _swa_attention(q, k_sh, v_sh, b_sinks[li])
                w_o, lead = w_bo, (li,)
            xs, sumsq, (h,) = _matmul_residual(o, w_o, lead, xs, 1.0, 1024, 512,
                                               [ffn_norm[layer, 1]])
            act, w_down = _swiglu_up(h, sumsq, w_in, w_out, (layer, 1))
            xs, sumsq, hs = _matmul_residual(act, w_down, (), xs, 0.5, 512, 512,
                                             gains_entering(layer + 1))
        outs.append(_final_norm(xs, sumsq, final_norm))
    return outs[0].reshape(b, s, d) if b == 1 else jnp.stack(outs, axis=0)
```

```python
import functools

import jax
import jax.numpy as jnp
import numpy as np
from jax import lax
from jax.experimental import pallas as pl
from jax.experimental.pallas import tpu as pltpu

F32 = jnp.float32
BF16 = jnp.bfloat16

RMS_EPS = 1e-6
ROPE_THETA = 10000.0
NEG = -1e30
LOG2_E = 1.4426950408889634
TAKEN = -3e38

A_HEAD_DIM = 128
A_GROUP = 4
MOBA_BLOCK = 256
MOBA_TOPK = 3

B_HEAD_DIM = 64
B_GROUP = 8
SWA_BLOCK = 128
SWA_STEP_BLOCKS = 8

LANES = 128
MXU_COLS = 256
BF16_SUBLANES = 16
UP_TILE_N = 256
V7X_VMEM_CAP = 60000 * 1024

_NT = (((1,), (1,)), ((), ()))
_TN = (((0,), (0,)), ((), ()))


def _params(semantics, block_bytes, scratch_bytes=0):
    need = 2 * block_bytes + scratch_bytes
    return pltpu.CompilerParams(
        dimension_semantics=semantics,
        vmem_limit_bytes=int(min(V7X_VMEM_CAP, max(need, 16 * 1024 * 1024))),
    )


def _lane_tile_sum(x):
    total = x[:, 0:LANES]
    for t in range(1, x.shape[1] // LANES):
        total = total + x[:, t * LANES:(t + 1) * LANES]
    return total


def _row_factor(sumsq_ref, width):
    mean = jnp.sum(sumsq_ref[...], axis=-1, keepdims=True) * (1.0 / width)
    return jnp.broadcast_to(lax.rsqrt(mean + RMS_EPS), sumsq_ref.shape)


def _prenorm_kernel(x_ref, g_ref, sumsq_ref, *o_refs):
    x = x_ref[...]
    sumsq_ref[...] = _lane_tile_sum(x * x)
    for n, o_ref in enumerate(o_refs):
        o_ref[...] = (x * g_ref[n:n + 1, :]).astype(o_ref.dtype)


def _prenorm(x, gains):
    s, d = x.shape
    n = gains.shape[0]
    tm = min(512, s)
    row = pl.BlockSpec((tm, d), lambda i: (i, 0))
    outs = pl.pallas_call(
        _prenorm_kernel,
        grid=(s // tm,),
        in_specs=[row, pl.BlockSpec((n, d), lambda i: (0, 0))],
        out_specs=[pl.BlockSpec((tm, LANES), lambda i: (i, 0))] + [row] * n,
        out_shape=[jax.ShapeDtypeStruct((s, LANES), F32)] + [jax.ShapeDtypeStruct((s, d), BF16)] * n,
        compiler_params=_params(("parallel",), tm * d * (4 + 2 * n), tm * d * 4),
        name="prenorm",
    )(x, gains)
    return outs[0], outs[1:]


def _final_norm_kernel(x_ref, sumsq_ref, g_ref, o_ref):
    mean = jnp.sum(sumsq_ref[...], axis=-1, keepdims=True) * (1.0 / x_ref.shape[1])
    o_ref[...] = x_ref[...] * lax.rsqrt(mean + RMS_EPS) * g_ref[...]


def _final_norm(x, sumsq, gain):
    s, d = x.shape
    tm = min(512, s)
    row = pl.BlockSpec((tm, d), lambda i: (i, 0))
    return pl.pallas_call(
        _final_norm_kernel,
        grid=(s // tm,),
        in_specs=[row, pl.BlockSpec((tm, LANES), lambda i: (i, 0)),
                  pl.BlockSpec((1, d), lambda i: (0, 0))],
        out_specs=row,
        out_shape=jax.ShapeDtypeStruct((s, d), F32),
        compiler_params=_params(("parallel",), tm * d * 8, tm * d * 4),
        name="final_norm",
    )(x, sumsq, gain[None, :])


def _swiglu_up_kernel(h_ref, sumsq_ref, wg_ref, wu_ref, wd_ref, o_ref, wd_bf16_ref, r_ref):
    @pl.when(pl.program_id(1) == 0)
    def _row_factors():
        r_ref[...] = _row_factor(sumsq_ref, h_ref.shape[1])

    wd_bf16_ref[...] = wd_ref[...].astype(BF16)

    h = h_ref[...]
    r = r_ref[...]
    g = jnp.dot(h, wg_ref[...].astype(BF16), preferred_element_type=F32)
    u = jnp.dot(h, wu_ref[...].astype(BF16), preferred_element_type=F32)
    for t in range(g.shape[1] // LANES):
        lanes = slice(t * LANES, (t + 1) * LANES)
        gt = g[:, lanes] * r
        o_ref[:, lanes] = (gt * (1.0 / (1.0 + jnp.exp(-gt))) * (u[:, lanes] * r)).astype(o_ref.dtype)


def _swiglu_up(h, sumsq, w_in, w_down, lead):
    s, d = h.shape
    f = w_in.shape[-1] // 2
    tm = min(1024, s)
    tn = UP_TILE_N
    nj = f // tn
    squeezed = (None,) * len(lead)
    n_steps = (s // tm) * nj
    share = f // n_steps
    assert share * n_steps == f and share % BF16_SUBLANES == 0, (f, n_steps)
    return pl.pallas_call(
        _swiglu_up_kernel,
        grid=(s // tm, nj),
        in_specs=[
            pl.BlockSpec((tm, d), lambda i, j: (i, 0)),
            pl.BlockSpec((tm, LANES), lambda i, j: (i, 0)),
            pl.BlockSpec(squeezed + (d, tn), lambda i, j: lead + (0, j)),
            pl.BlockSpec(squeezed + (d, tn), lambda i, j: lead + (0, j + nj)),
            pl.BlockSpec(squeezed + (share, d), lambda i, j: lead + (i * nj + j, 0)),
        ],
        out_specs=[pl.BlockSpec((tm, tn), lambda i, j: (i, j)),
                   pl.BlockSpec((share, d), lambda i, j: (i * nj + j, 0))],
        out_shape=[jax.ShapeDtypeStruct((s, f), BF16), jax.ShapeDtypeStruct((f, d), BF16)],
        scratch_shapes=[pltpu.VMEM((tm, LANES), F32)],
        compiler_params=_params(
            ("parallel", "arbitrary"),
            tm * d * 2 + 2 * d * tn * w_in.dtype.itemsize + tm * tn * 2 + share * d * 6,
            4 * tm * tn * 4 + 2 * d * tn * 2),
        name="swiglu_up",
    )(h, sumsq, w_in, w_in, w_down)


def _matmul_residual_kernel(a_ref, w_ref, r_ref, *refs, alpha, n_gains):
    if n_gains:
        g_ref, o_ref, sumsq_ref, *h_refs = refs
    else:
        o_ref, sumsq_ref = refs
        h_refs = []
    acc = jnp.dot(a_ref[...], w_ref[...].astype(BF16), preferred_element_type=F32)
    x_new = r_ref[...] + alpha * acc
    o_ref[...] = x_new
    part = _lane_tile_sum(x_new * x_new)

    @pl.when(pl.program_id(1) == 0)
    def _first():
        sumsq_ref[...] = part

    @pl.when(pl.program_id(1) != 0)
    def _rest():
        sumsq_ref[...] += part

    for n, h_ref in enumerate(h_refs):
        h_ref[...] = (x_new * g_ref[n:n + 1, :]).astype(h_ref.dtype)


def _matmul_residual(a, w, lead, res, alpha, tm, tn, next_gains):
    s, k = a.shape
    n = w.shape[-1]
    ng = len(next_gains)
    tm, tn = min(tm, s), min(tn, n)
    squeezed = (None,) * len(lead)
    tile = pl.BlockSpec((tm, tn), lambda i, j: (i, j))
    in_specs = [
        pl.BlockSpec((tm, k), lambda i, j: (i, 0)),
        pl.BlockSpec(squeezed + (k, tn), lambda i, j: lead + (0, j)),
        tile,
    ]
    args = [a, w, res]
    if ng:
        in_specs.append(pl.BlockSpec((ng, tn), lambda i, j: (0, j)))
        args.append(jnp.stack(next_gains))
    outs = pl.pallas_call(
        functools.partial(_matmul_residual_kernel, alpha=alpha, n_gains=ng),
        grid=(s // tm, n // tn),
        in_specs=in_specs,
        out_specs=[tile, pl.BlockSpec((tm, LANES), lambda i, j: (i, 0))] + [tile] * ng,
        out_shape=[jax.ShapeDtypeStruct((s, n), F32), jax.ShapeDtypeStruct((s, LANES), F32)]
        + [jax.ShapeDtypeStruct((s, n), BF16)] * ng,
        compiler_params=_params(
            ("parallel", "arbitrary"),
            tm * k * 2 + k * tn * w.dtype.itemsize + 2 * tm * tn * 4 + tm * LANES * 4
            + ng * tm * tn * 2,
            3 * tm * tn * 4 + k * tn * 2),
        name="matmul_residual",
    )(*args)
    return outs[0], outs[1], outs[2:]


def _rope_tables(seq, dim):
    half = dim // 2
    inv = 1.0 / (ROPE_THETA ** (jnp.arange(0, dim, 2, dtype=F32) / dim))
    ang = jnp.arange(seq, dtype=F32)[:, None] * inv[None, :]
    cos, sin = jnp.cos(ang), jnp.sin(ang)
    zero = jnp.zeros_like(sin)
    reps = LANES // dim
    cos_t = jnp.tile(jnp.concatenate([cos, cos], axis=-1), (1, reps))
    s_lo = jnp.tile(jnp.concatenate([-sin, zero], axis=-1), (1, reps))
    s_hi = jnp.tile(jnp.concatenate([zero, sin], axis=-1), (1, reps))
    return cos_t, s_lo, s_hi


def _proj_kernel(h_ref, sumsq_ref, w_ref, *refs, half, scale, group_major):
    acc = jnp.dot(h_ref[...], w_ref[...].astype(BF16), preferred_element_type=F32)
    if half:
        cos_ref, slo_ref, shi_ref, o_ref = refs
        cos, s_lo, s_hi = cos_ref[...], slo_ref[...], shi_ref[...]
        one_roll = 2 * half == LANES
        if one_roll:
            s_both = s_lo + s_hi
    else:
        (o_ref,) = refs
    r = _row_factor(sumsq_ref, h_ref.shape[1]) * scale
    for g in range(acc.shape[1] // LANES):
        x = acc[:, g * LANES:(g + 1) * LANES] * r
        if half and one_roll:
            x = x * cos + pltpu.roll(x, half, axis=1) * s_both
        elif half:
            x = (x * cos + pltpu.roll(x, LANES - half, axis=1) * s_lo
                 + pltpu.roll(x, half, axis=1) * s_hi)
        if group_major:
            o_ref[g] = x.astype(o_ref.dtype)
        else:
            o_ref[:, g * LANES:(g + 1) * LANES] = x.astype(o_ref.dtype)


def _proj(h, sumsq, w, col0, ncols, *, rope=None, half=0, scale=1.0, group_major=False):
    s, k = h.shape
    tm = min(1024, s)
    tn = min(512, ncols)
    j0 = col0 // tn
    in_specs = [
        pl.BlockSpec((tm, k), lambda i, j: (i, 0)),
        pl.BlockSpec((tm, LANES), lambda i, j: (i, 0)),
        pl.BlockSpec((k, tn), lambda i, j: (0, j + j0)),
    ]
    args = [h, sumsq, w]
    if half:
        tab = pl.BlockSpec((tm, LANES), lambda i, j: (i, 0))
        in_specs += [tab, tab, tab]
        args += list(rope)
    if group_major:
        gpt = tn // LANES
        out_spec = pl.BlockSpec((gpt, tm, LANES), lambda i, j: (j, i, 0))
        out_shape = jax.ShapeDtypeStruct((ncols // LANES, s, LANES), BF16)
    else:
        out_spec = pl.BlockSpec((tm, tn), lambda i, j: (i, j))
        out_shape = jax.ShapeDtypeStruct((s, ncols), BF16)
    return pl.pallas_call(
        functools.partial(_proj_kernel, half=half, scale=scale, group_major=group_major),
        grid=(s // tm, ncols // tn),
        in_specs=in_specs,
        out_specs=out_spec,
        out_shape=out_shape,
        compiler_params=_params(
            ("parallel", "arbitrary"),
            tm * k * 2 + k * tn * w.dtype.itemsize + 4 * tm * LANES * 4 + tm * tn * 2,
            3 * tm * tn * 4 + k * tn * 2),
        name="proj",
    )(*args)


def _moba_kernel(q_ref, k_ref, vt_ref, o_ref, kmean_ref, qa_ref, sa_ref, sb_ref, m_ref, acc_ref):
    group, blk, dh = q_ref.shape
    nb = vt_ref.shape[0]
    rows = group * blk
    i = pl.program_id(1)

    @pl.when(i == 0)
    def _block_means():
        def body(j, carry):
            kj = k_ref[pl.ds(pl.multiple_of(j * blk, blk), blk), 0:dh].astype(F32)
            kmean_ref[pl.ds(j, 1), :] = jnp.sum(kj, axis=0, keepdims=True) * (1.0 / blk)
            return carry
        lax.fori_loop(0, nb, body, 0)

    q_t = q_ref[...].reshape(rows, dh).astype(F32).T.astype(BF16)
    qa_ref[0:dh, :] = q_t

    own = pl.multiple_of(i * blk, blk)
    s_own = jnp.dot(k_ref[pl.ds(own, blk), 0:dh], q_t, preferred_element_type=F32)

    km = kmean_ref[...]
    km_hi = km.astype(BF16)
    km_lo = (km - km_hi.astype(F32)).astype(BF16)
    gate = (jnp.dot(km_hi, q_t, preferred_element_type=F32)
            + jnp.dot(km_lo, q_t, preferred_element_type=F32))
    bidx = lax.broadcasted_iota(jnp.int32, (nb, rows), 0)
    past = bidx < i
    bidx = bidx.astype(F32)
    gate = jnp.where(past, gate, NEG)
    bias = jnp.full((nb, rows), NEG, F32)
    for _ in range(min(MOBA_TOPK, nb)):
        top = jnp.max(gate, axis=0, keepdims=True)
        first = jnp.min(jnp.where(gate == top, bidx, float(nb)), axis=0, keepdims=True)
        pick = bidx == first
        bias = jnp.where(pick & past, 0.0, bias)
        gate = jnp.where(pick, TAKEN, gate)
    qa_ref[dh:dh + nb, :] = bias.astype(BF16)
    if nb < LANES:
        qa_ref[dh + nb:dh + LANES, :] = jnp.zeros((LANES - nb, rows), BF16)

    s = s_own
    kpos = lax.broadcasted_iota(jnp.int32, (blk, rows), 0)
    qpos = lax.broadcasted_iota(jnp.int32, (blk, rows), 1) & (blk - 1)
    s = jnp.where(kpos <= qpos, s, NEG)
    m = jnp.max(s, axis=0, keepdims=True)
    m_ref[...] = m
    acc_ref[...] = jnp.dot(vt_ref[i], jnp.exp2(s - m).astype(BF16), preferred_element_type=F32)

    def scores(j, s_ref):
        off = pl.multiple_of(jnp.minimum(j, nb - 1) * blk, blk)
        s_ref[...] = jnp.dot(k_ref[pl.ds(off, blk), :], qa_ref[...], preferred_element_type=F32)

    reread = pl.multiple_of(jnp.minimum(i, 0) * blk, blk)

    def update(s_ref, j):
        v_t = vt_ref[jnp.minimum(j, nb - 1)]
        for c in range(rows // MXU_COLS):
            cols = slice(c * MXU_COLS, (c + 1) * MXU_COLS)
            s = s_ref[pl.ds(reread, blk), cols]
            m_old = m_ref[:, cols]
            m_new = jnp.maximum(m_old, jnp.max(s, axis=0, keepdims=True))
            m_ref[:, cols] = m_new
            p = jnp.exp2(s - m_new).astype(BF16)
            acc_ref[:, cols] = jnp.exp2(m_old - m_new) * acc_ref[:, cols] + jnp.dot(
                v_t, p, preferred_element_type=F32)

    scores(0, sa_ref)

    def pair(j):
        scores(j + 1, sb_ref)
        update(sa_ref, j)
        scores(j + 2, sa_ref)
        update(sb_ref, j + 1)

    def quad_trip(t, carry):
        pair(4 * t)
        pair(4 * t + 2)
        return carry
    def oct_trip(t, carry):
        pair(8 * t)
        pair(8 * t + 2)
        pair(8 * t + 4)
        pair(8 * t + 6)
        return carry
    n_octs = i // 8
    lax.fori_loop(0, n_octs, oct_trip, 0)
    n_quads = i // 4
    lax.fori_loop(2 * n_octs, n_quads, quad_trip, 0)

    def pair_trip(t, carry):
        pair(4 * n_quads + 2 * t)
        return carry
    lax.fori_loop(0, (i - 4 * n_quads + 1) // 2, pair_trip, 0)

    out_t = acc_ref[0:dh, :] / acc_ref[dh:dh + 1, :]
    for g in range(group):
        o_ref[:, g * dh:(g + 1) * dh] = out_t[:, g * blk:(g + 1) * blk].T.astype(o_ref.dtype)


def _moba_attention(q, k, v):
    h, s, dh = q.shape
    hkv = k.shape[0]
    group = h // hkv
    blk = MOBA_BLOCK
    nb = s // blk
    rows = group * blk
    assert nb <= LANES and dh == LANES
    block_of_key = jnp.arange(s, dtype=jnp.int32)[:, None] // blk
    onehot = (block_of_key == jnp.arange(LANES, dtype=jnp.int32)[None, :]).astype(BF16)
    k_aug = jnp.concatenate([k, jnp.broadcast_to(onehot, (hkv, s, LANES))], axis=-1)
    v_t = v.reshape(hkv, nb, blk, dh).transpose(0, 1, 3, 2)
    ones_rows = jnp.zeros((hkv, nb, BF16_SUBLANES, blk), BF16).at[:, :, 0, :].set(1.0)
    vt_aug = jnp.concatenate([v_t, ones_rows], axis=2)
    vrows = dh + BF16_SUBLANES
    return pl.pallas_call(
        _moba_kernel,
        grid=(hkv, nb),
        in_specs=[
            pl.BlockSpec((group, blk, dh), lambda hh, i: (hh, i, 0)),
            pl.BlockSpec((None, s, 2 * LANES), lambda hh, i: (hh, 0, 0)),
            pl.BlockSpec((None, nb, vrows, blk), lambda hh, i: (hh, 0, 0, 0)),
        ],
        out_specs=pl.BlockSpec((blk, group * dh), lambda hh, i: (i, hh)),
        out_shape=jax.ShapeDtypeStruct((s, h * dh), BF16),
        scratch_shapes=[
            pltpu.VMEM((nb, dh), F32),
            pltpu.VMEM((2 * LANES, rows), BF16),
            pltpu.VMEM((blk, rows), F32),
            pltpu.VMEM((blk, rows), F32),
            pltpu.VMEM((1, rows), F32),
            pltpu.VMEM((vrows, rows), F32),
        ],
        compiler_params=_params(
            ("arbitrary", "arbitrary"),
            rows * dh * 2 + s * 2 * LANES * 2 + nb * vrows * blk * 2 + blk * group * dh * 2,
            (2 * blk + vrows + LANES + 1) * rows * 4 + 6 * blk * rows * 4),
        name="moba_attention",
    )(q, k_aug, vt_aug)


def _swa_kernel(q_ref, kp_ref, ko_ref, vp_ref, vo_ref, sink_ref, bias_ref, o_ref, *, nq):
    pairs, span, _ = q_ref.shape
    blk = span // nq
    rows = pairs * blk
    odd = (pl.program_id(0) % 2) == 1
    first_step = pl.program_id(1) == 0

    lane = lax.broadcasted_iota(jnp.int32, (span + blk, LANES), 1)
    lo = lane < B_HEAD_DIM

    def strip(prev_ref, own_ref):
        x = jnp.concatenate([prev_ref[...], own_ref[...]], axis=0)
        swapped = pltpu.roll(x, B_HEAD_DIM, axis=1)
        mine_lo = jnp.where(odd, swapped, x)
        mine_hi = jnp.where(odd, x, swapped)
        return mine_lo, mine_hi

    k_lo, k_hi = strip(kp_ref, ko_ref)
    v_lo, v_hi = strip(vp_ref, vo_ref)
    zero = jnp.zeros_like(k_lo)
    k_both = jnp.where(lo, k_lo, k_hi)
    v_first = jnp.where(lo, v_lo, zero)
    v_second = jnp.where(lo, zero, v_hi)

    qlane = lax.broadcasted_iota(jnp.int32, (rows, LANES), 1)
    q_is_first = qlane < B_HEAD_DIM

    chains = []
    for u in range(nq):
        band = slice(u * blk, (u + 2) * blk)
        q = q_ref[:, u * blk:(u + 1) * blk, :].reshape(rows, LANES)
        qzero = jnp.zeros_like(q)
        bias = bias_ref[1] if u else bias_ref[jnp.where(first_step, 0, 1)]
        for q_half, v_half, sink in (
                (jnp.where(q_is_first, q, qzero), v_first, sink_ref[0:1, :]),
                (jnp.where(q_is_first, qzero, q), v_second, sink_ref[1:2, :])):
            s = lax.dot_general(k_both[band], q_half, _NT, preferred_element_type=F32)
            chains.append((s + bias, v_half[band], sink))
    probs = []
    for s, v_band, sink in chains:
        m = jnp.maximum(jnp.max(s, axis=0, keepdims=True), sink)
        p = jnp.exp2(s - m)
        denom = jnp.sum(p, axis=0, keepdims=True) + jnp.exp2(sink - m)
        probs.append((p.astype(BF16), v_band, denom))
    outs = [lax.dot_general(v_band, p, _TN, preferred_element_type=F32) / denom
            for p, v_band, denom in probs]
    for u in range(nq):
        out_t = outs[2 * u] + outs[2 * u + 1]
        for pr in range(pairs):
            o_ref[u * blk:(u + 1) * blk, pr * LANES:(pr + 1) * LANES] = (
                out_t[:, pr * blk:(pr + 1) * blk].T.astype(o_ref.dtype))


def _swa_attention(q, k, v, sinks):
    npair, s, _ = q.shape
    hkv = k.shape[1] // B_HEAD_DIM
    pairs = npair // hkv
    blk = SWA_BLOCK
    nq = SWA_STEP_BLOCKS
    span = nq * blk
    rows = pairs * blk
    sink_rows = jnp.repeat(
        (sinks.astype(F32) * LOG2_E).reshape(hkv, pairs, 2).transpose(0, 2, 1), blk, axis=-1)
    kk = np.arange(2 * blk)[:, None]
    tt = np.arange(rows)[None, :] % blk
    in_window = (kk > tt) & (kk <= tt + blk)
    band_bias = jnp.asarray(
        np.where(np.stack([in_window & (kk >= blk), in_window]), 0.0, NEG), dtype=F32)
    prev = pl.BlockSpec((blk, LANES), lambda hh, n: (jnp.maximum(n * nq - 1, 0), hh // 2))
    own = pl.BlockSpec((span, LANES), lambda hh, n: (n, hh // 2))
    return pl.pallas_call(
        functools.partial(_swa_kernel, nq=nq),
        grid=(hkv, s // span),
        in_specs=[
            pl.BlockSpec((pairs, span, LANES), lambda hh, n: (hh, n, 0)),
            prev, own, prev, own,
            pl.BlockSpec((None, 2, rows), lambda hh, n: (hh, 0, 0)),
            pl.BlockSpec((2, 2 * blk, rows), lambda hh, n: (0, 0, 0)),
        ],
        out_specs=pl.BlockSpec((span, pairs * LANES), lambda hh, n: (n, hh)),
        out_shape=jax.ShapeDtypeStruct((s, npair * LANES), BF16),
        compiler_params=_params(
            ("parallel", "parallel"),
            2 * pairs * span * LANES * 2 + 2 * (span + blk) * LANES * 2 + 2 * rows * 4
            + 4 * blk * rows * 4,
            12 * blk * rows * 4),
        name="swa_attention",
    )(q, k, k, v, v, sink_rows, band_bias)


def kernel(x, ffn_norm, ffn_w_in, ffn_w_out, attn_norm, a_w_qkv, a_w_o, kv_norm,
           w_kv_shared, b_w_q, b_sinks, b_w_o, final_norm):
    b, s, d = x.shape
    depth = ffn_norm.shape[0]
    n_a = a_w_qkv.shape[0]
    a_q = a_w_o.shape[1]
    a_kv = (a_w_qkv.shape[2] - a_q) // 2
    b_kv = w_kv_shared.shape[1] // 2
    rope_a = _rope_tables(s, A_HEAD_DIM)
    rope_b = _rope_tables(s, B_HEAD_DIM)
    w_in, w_out = ffn_w_in, ffn_w_out
    w_qkv, w_ao = a_w_qkv.astype(BF16), a_w_o.astype(BF16)
    w_kv, w_bq, w_bo = w_kv_shared.astype(BF16), b_w_q.astype(BF16), b_w_o.astype(BF16)

    def gains_entering(layer):
        if layer == depth:
            return []
        return ([kv_norm] if layer == n_a else []) + [ffn_norm[layer, 0]]

    outs = []
    for bi in range(b):
        xs = x.reshape(s, d) if b == 1 else x[bi]
        sumsq, hs = _prenorm(xs, jnp.stack(gains_entering(0)))
        k_sh = v_sh = None
        for layer in range(depth):
            hs = list(hs)
            if layer == n_a:
                h_kv = hs.pop(0)
                k_sh = _proj(h_kv, sumsq, w_kv, 0, b_kv, rope=rope_b, half=B_HEAD_DIM // 2)
                v_sh = _proj(h_kv, sumsq, w_kv, b_kv, b_kv)
            act, w_down = _swiglu_up(hs[0], sumsq, w_in, w_out, (layer, 0))
            xs, sumsq, (h,) = _matmul_residual(act, w_down, (), xs, 0.5, 512, 512,
                                               [attn_norm[layer]])
            if layer < n_a:
                w = w_qkv[layer]
                q = _proj(h, sumsq, w, 0, a_q, rope=rope_a, half=A_HEAD_DIM // 2,
                          scale=A_HEAD_DIM ** -0.5 * LOG2_E, group_major=True)
                k = _proj(h, sumsq, w, a_q, a_kv, rope=rope_a, half=A_HEAD_DIM // 2,
                          group_major=True)
                v = _proj(h, sumsq, w, a_q + a_kv, a_kv, group_major=True)
                o = _moba_attention(q, k, v)
                w_o, lead = w_ao, (layer,)
            else:
                li = layer - n_a
                q = _proj(h, sumsq, w_bq[li], 0, w_bq.shape[2], rope=rope_b,
                          half=B_HEAD_DIM // 2, scale=B_HEAD_DIM ** -0.5 * LOG2_E, group_major=True)
                o = _swa_attention(q, k_sh, v_sh, b_sinks[li])
                w_o, lead = w_bo, (li,)
            xs, sumsq, (h,) = _matmul_residual(o, w_o, lead, xs, 1.0, 1024, 512,
                                               [ffn_norm[layer, 1]])
            act, w_down = _swiglu_up(h, sumsq, w_in, w_out, (layer, 1))
            xs, sumsq, hs = _matmul_residual(act, w_down, (), xs, 0.5, 512, 512,
                                             gains_entering(layer + 1))
        outs.append(_final_norm(xs, sumsq, final_norm))
    return outs[0].reshape(b, s, d) if b == 1 else jnp.stack(outs, axis=0)
```

```python
import functools

import jax
import jax.numpy as jnp
import numpy as np
from jax import lax
from jax.experimental import pallas as pl
from jax.experimental.pallas import tpu as pltpu

F32 = jnp.float32
BF16 = jnp.bfloat16

RMS_EPS = 1e-6
ROPE_THETA = 10000.0
NEG = -1e30
LOG2_E = 1.4426950408889634
TAKEN = -3e38

A_HEAD_DIM = 128
MOBA_BLOCK = 256
MOBA_TOPK = 3

B_HEAD_DIM = 64
SWA_BLOCK = 128
SWA_STEP_BLOCKS = 8

LANES = 128
MXU_COLS = 256
BF16_SUBLANES = 16
UP_TILE_N = 256
V7X_VMEM_CAP = 60000 * 1024

_NT = (((1,), (1,)), ((), ()))
_TN = (((0,), (0,)), ((), ()))


def _params(semantics, block_bytes, scratch_bytes=0):
    need = 2 * block_bytes + scratch_bytes
    return pltpu.CompilerParams(
        dimension_semantics=semantics,
        vmem_limit_bytes=int(min(V7X_VMEM_CAP, max(need, 16 * 1024 * 1024))),
    )


def _lane_tile_sum(x):
    total = x[:, 0:LANES]
    for t in range(1, x.shape[1] // LANES):
        total = total + x[:, t * LANES:(t + 1) * LANES]
    return total


def _row_factor(sumsq_ref, width):
    mean = jnp.sum(sumsq_ref[...], axis=-1, keepdims=True) * (1.0 / width)
    return jnp.broadcast_to(lax.rsqrt(mean + RMS_EPS), sumsq_ref.shape)


def _prenorm_kernel(x_ref, g_ref, sumsq_ref, *o_refs):
    x = x_ref[...]
    sumsq_ref[...] = _lane_tile_sum(x * x)
    for n, o_ref in enumerate(o_refs):
        o_ref[...] = (x * g_ref[n:n + 1, :]).astype(o_ref.dtype)


def _prenorm(x, gains):
    s, d = x.shape
    n = gains.shape[0]
    tm = min(512, s)
    row = pl.BlockSpec((tm, d), lambda i: (i, 0))
    outs = pl.pallas_call(
        _prenorm_kernel,
        grid=(s // tm,),
        in_specs=[row, pl.BlockSpec((n, d), lambda i: (0, 0))],
        out_specs=[pl.BlockSpec((tm, LANES), lambda i: (i, 0))] + [row] * n,
        out_shape=[jax.ShapeDtypeStruct((s, LANES), F32)] + [jax.ShapeDtypeStruct((s, d), BF16)] * n,
        compiler_params=_params(("parallel",), tm * d * (4 + 2 * n), tm * d * 4),
        name="prenorm",
    )(x, gains)
    return outs[0], outs[1:]


def _final_norm_kernel(x_ref, sumsq_ref, g_ref, o_ref):
    mean = jnp.sum(sumsq_ref[...], axis=-1, keepdims=True) * (1.0 / x_ref.shape[1])
    o_ref[...] = x_ref[...] * lax.rsqrt(mean + RMS_EPS) * g_ref[...]


def _final_norm(x, sumsq, gain):
    s, d = x.shape
    tm = min(512, s)
    row = pl.BlockSpec((tm, d), lambda i: (i, 0))
    return pl.pallas_call(
        _final_norm_kernel,
        grid=(s // tm,),
        in_specs=[row, pl.BlockSpec((tm, LANES), lambda i: (i, 0)),
                  pl.BlockSpec((1, d), lambda i: (0, 0))],
        out_specs=row,
        out_shape=jax.ShapeDtypeStruct((s, d), F32),
        compiler_params=_params(("parallel",), tm * d * 8, tm * d * 4),
        name="final_norm",
    )(x, sumsq, gain[None, :])


def _swiglu_up_kernel(h_ref, sumsq_ref, wg_ref, wu_ref, wd_ref, o_ref, wd_bf16_ref, r_ref):
    @pl.when(pl.program_id(1) == 0)
    def _row_factors():
        r_ref[...] = _row_factor(sumsq_ref, h_ref.shape[1])

    wd_bf16_ref[...] = wd_ref[...].astype(BF16)

    h = h_ref[...]
    r = r_ref[...]
    g = jnp.dot(h, wg_ref[...].astype(BF16), preferred_element_type=F32)
    u = jnp.dot(h, wu_ref[...].astype(BF16), preferred_element_type=F32)
    for t in range(g.shape[1] // LANES):
        lanes = slice(t * LANES, (t + 1) * LANES)
        gt = g[:, lanes] * r
        o_ref[:, lanes] = (gt * (1.0 / (1.0 + jnp.exp(-gt))) * (u[:, lanes] * r)).astype(o_ref.dtype)


def _swiglu_up(h, sumsq, w_in, w_down, lead):
    s, d = h.shape
    f = w_in.shape[-1] // 2
    tm = min(1024, s)
    tn = UP_TILE_N
    nj = f // tn
    squeezed = (None,) * len(lead)
    n_steps = (s // tm) * nj
    share = f // n_steps
    assert share * n_steps == f and share % BF16_SUBLANES == 0, (f, n_steps)
    return pl.pallas_call(
        _swiglu_up_kernel,
        grid=(s // tm, nj),
        in_specs=[
            pl.BlockSpec((tm, d), lambda i, j: (i, 0)),
            pl.BlockSpec((tm, LANES), lambda i, j: (i, 0)),
            pl.BlockSpec(squeezed + (d, tn), lambda i, j: lead + (0, j)),
            pl.BlockSpec(squeezed + (d, tn), lambda i, j: lead + (0, j + nj)),
            pl.BlockSpec(squeezed + (share, d), lambda i, j: lead + (i * nj + j, 0)),
        ],
        out_specs=[pl.BlockSpec((tm, tn), lambda i, j: (i, j)),
                   pl.BlockSpec((share, d), lambda i, j: (i * nj + j, 0))],
        out_shape=[jax.ShapeDtypeStruct((s, f), BF16), jax.ShapeDtypeStruct((f, d), BF16)],
        scratch_shapes=[pltpu.VMEM((tm, LANES), F32)],
        compiler_params=_params(
            ("parallel", "arbitrary"),
            tm * d * 2 + 2 * d * tn * w_in.dtype.itemsize + tm * tn * 2 + share * d * 6,
            4 * tm * tn * 4 + 2 * d * tn * 2),
        name="swiglu_up",
    )(h, sumsq, w_in, w_in, w_down)


def _matmul_residual_kernel(a_ref, w_ref, r_ref, *refs, alpha, n_gains):
    if n_gains:
        g_ref, o_ref, sumsq_ref, *h_refs = refs
    else:
        o_ref, sumsq_ref = refs
        h_refs = []
    acc = jnp.dot(a_ref[...], w_ref[...].astype(BF16), preferred_element_type=F32)
    x_new = r_ref[...] + alpha * acc
    o_ref[...] = x_new
    part = _lane_tile_sum(x_new * x_new)

    @pl.when(pl.program_id(1) == 0)
    def _first():
        sumsq_ref[...] = part

    @pl.when(pl.program_id(1) != 0)
    def _rest():
        sumsq_ref[...] += part

    for n, h_ref in enumerate(h_refs):
        h_ref[...] = (x_new * g_ref[n:n + 1, :]).astype(h_ref.dtype)


def _matmul_residual(a, w, lead, res, alpha, tm, tn, next_gains):
    s, k = a.shape
    n = w.shape[-1]
    ng = len(next_gains)
    tm, tn = min(tm, s), min(tn, n)
    squeezed = (None,) * len(lead)
    tile = pl.BlockSpec((tm, tn), lambda i, j: (i, j))
    in_specs = [
        pl.BlockSpec((tm, k), lambda i, j: (i, 0)),
        pl.BlockSpec(squeezed + (k, tn), lambda i, j: lead + (0, j)),
        tile,
    ]
    args = [a, w, res]
    if ng:
        in_specs.append(pl.BlockSpec((ng, tn), lambda i, j: (0, j)))
        args.append(jnp.stack(next_gains))
    outs = pl.pallas_call(
        functools.partial(_matmul_residual_kernel, alpha=alpha, n_gains=ng),
        grid=(s // tm, n // tn),
        in_specs=in_specs,
        out_specs=[tile, pl.BlockSpec((tm, LANES), lambda i, j: (i, 0))] + [tile] * ng,
        out_shape=[jax.ShapeDtypeStruct((s, n), F32), jax.ShapeDtypeStruct((s, LANES), F32)]
        + [jax.ShapeDtypeStruct((s, n), BF16)] * ng,
        compiler_params=_params(
            ("parallel", "arbitrary"),
            tm * k * 2 + k * tn * w.dtype.itemsize + 2 * tm * tn * 4 + tm * LANES * 4
            + ng * tm * tn * 2,
            3 * tm * tn * 4 + k * tn * 2),
        name="matmul_residual",
    )(*args)
    return outs[0], outs[1], outs[2:]


def _rope_tables(seq, dim):
    half = dim // 2
    inv = 1.0 / (ROPE_THETA ** (jnp.arange(0, dim, 2, dtype=F32) / dim))
    ang = jnp.arange(seq, dtype=F32)[:, None] * inv[None, :]
    cos, sin = jnp.cos(ang), jnp.sin(ang)
    zero = jnp.zeros_like(sin)
    reps = LANES // dim
    cos_t = jnp.tile(jnp.concatenate([cos, cos], axis=-1), (1, reps))
    s_lo = jnp.tile(jnp.concatenate([-sin, zero], axis=-1), (1, reps))
    s_hi = jnp.tile(jnp.concatenate([zero, sin], axis=-1), (1, reps))
    return cos_t, s_lo, s_hi


def _proj_kernel(h_ref, sumsq_ref, w_ref, *refs, half, scale, group_major):
    acc = jnp.dot(h_ref[...], w_ref[...].astype(BF16), preferred_element_type=F32)
    if half:
        cos_ref, slo_ref, shi_ref, o_ref = refs
        cos, s_lo, s_hi = cos_ref[...], slo_ref[...], shi_ref[...]
        one_roll = 2 * half == LANES
        if one_roll:
            s_both = s_lo + s_hi
    else:
        (o_ref,) = refs
    r = _row_factor(sumsq_ref, h_ref.shape[1]) * scale
    for g in range(acc.shape[1] // LANES):
        x = acc[:, g * LANES:(g + 1) * LANES] * r
        if half and one_roll:
            x = x * cos + pltpu.roll(x, half, axis=1) * s_both
        elif half:
            x = (x * cos + pltpu.roll(x, LANES - half, axis=1) * s_lo
                 + pltpu.roll(x, half, axis=1) * s_hi)
        if group_major:
            o_ref[g] = x.astype(o_ref.dtype)
        else:
            o_ref[:, g * LANES:(g + 1) * LANES] = x.astype(o_ref.dtype)


def _proj(h, sumsq, w, col0, ncols, *, rope=None, half=0, scale=1.0, group_major=False):
    s, k = h.shape
    tm = min(1024, s)
    tn = min(512, ncols)
    j0 = col0 // tn
    in_specs = [
        pl.BlockSpec((tm, k), lambda i, j: (i, 0)),
        pl.BlockSpec((tm, LANES), lambda i, j: (i, 0)),
        pl.BlockSpec((k, tn), lambda i, j: (0, j + j0)),
    ]
    args = [h, sumsq, w]
    if half:
        tab = pl.BlockSpec((tm, LANES), lambda i, j: (i, 0))
        in_specs += [tab, tab, tab]
        args += list(rope)
    if group_major:
        gpt = tn // LANES
        out_spec = pl.BlockSpec((gpt, tm, LANES), lambda i, j: (j, i, 0))
        out_shape = jax.ShapeDtypeStruct((ncols // LANES, s, LANES), BF16)
    else:
        out_spec = pl.BlockSpec((tm, tn), lambda i, j: (i, j))
        out_shape = jax.ShapeDtypeStruct((s, ncols), BF16)
    return pl.pallas_call(
        functools.partial(_proj_kernel, half=half, scale=scale, group_major=group_major),
        grid=(s // tm, ncols // tn),
        in_specs=in_specs,
        out_specs=out_spec,
        out_shape=out_shape,
        compiler_params=_params(
            ("parallel", "arbitrary"),
            tm * k * 2 + k * tn * w.dtype.itemsize + 4 * tm * LANES * 4 + tm * tn * 2,
            3 * tm * tn * 4 + k * tn * 2),
        name="proj",
    )(*args)


def _moba_kernel(q_ref, k_ref, vt_ref, o_ref, kmean_ref, qa_ref, sa_ref, sb_ref, m_ref, acc_ref):
    group, blk, dh = q_ref.shape
    nb = vt_ref.shape[0]
    rows = group * blk
    i = pl.program_id(1)

    @pl.when(i == 0)
    def _block_means():
        def body(j, carry):
            kj = k_ref[pl.ds(pl.multiple_of(j * blk, blk), blk), 0:dh].astype(F32)
            kmean_ref[pl.ds(j, 1), :] = jnp.sum(kj, axis=0, keepdims=True) * (1.0 / blk)
            return carry
        lax.fori_loop(0, nb, body, 0)

    q_t = q_ref[...].reshape(rows, dh).astype(F32).T.astype(BF16)
    qa_ref[0:dh, :] = q_t

    own = pl.multiple_of(i * blk, blk)
    s_own = jnp.dot(k_ref[pl.ds(own, blk), 0:dh], q_t, preferred_element_type=F32)

    km = kmean_ref[...]
    km_hi = km.astype(BF16)
    km_lo = (km - km_hi.astype(F32)).astype(BF16)
    gate = (jnp.dot(km_hi, q_t, preferred_element_type=F32)
            + jnp.dot(km_lo, q_t, preferred_element_type=F32))
    bidx = lax.broadcasted_iota(jnp.int32, (nb, rows), 0)
    past = bidx < i
    bidx = bidx.astype(F32)
    gate = jnp.where(past, gate, NEG)
    bias = jnp.full((nb, rows), NEG, F32)
    for _ in range(min(MOBA_TOPK, nb)):
        top = jnp.max(gate, axis=0, keepdims=True)
        first = jnp.min(jnp.where(gate == top, bidx, float(nb)), axis=0, keepdims=True)
        pick = bidx == first
        bias = jnp.where(pick & past, 0.0, bias)
        gate = jnp.where(pick, TAKEN, gate)
    qa_ref[dh:dh + nb, :] = bias.astype(BF16)
    if nb < LANES:
        qa_ref[dh + nb:dh + LANES, :] = jnp.zeros((LANES - nb, rows), BF16)

    s = s_own
    kpos = lax.broadcasted_iota(jnp.int32, (blk, rows), 0)
    qpos = lax.broadcasted_iota(jnp.int32, (blk, rows), 1) & (blk - 1)
    s = jnp.where(kpos <= qpos, s, NEG)
    m = jnp.max(s, axis=0, keepdims=True)
    m_ref[...] = m
    acc_ref[...] = jnp.dot(vt_ref[i], jnp.exp2(s - m).astype(BF16), preferred_element_type=F32)

    def scores(j, s_ref):
        off = pl.multiple_of(jnp.minimum(j, nb - 1) * blk, blk)
        s_ref[...] = jnp.dot(k_ref[pl.ds(off, blk), :], qa_ref[...], preferred_element_type=F32)

    reread = pl.multiple_of(jnp.minimum(i, 0) * blk, blk)

    def update(s_ref, j):
        v_t = vt_ref[jnp.minimum(j, nb - 1)]
        for c in range(rows // MXU_COLS):
            cols = slice(c * MXU_COLS, (c + 1) * MXU_COLS)
            s = s_ref[pl.ds(reread, blk), cols]
            m_old = m_ref[:, cols]
            m_new = jnp.maximum(m_old, jnp.max(s, axis=0, keepdims=True))
            m_ref[:, cols] = m_new
            p = jnp.exp2(s - m_new).astype(BF16)
            acc_ref[:, cols] = jnp.exp2(m_old - m_new) * acc_ref[:, cols] + jnp.dot(
                v_t, p, preferred_element_type=F32)

    scores(0, sa_ref)

    def pair(j):
        scores(j + 1, sb_ref)
        update(sa_ref, j)
        scores(j + 2, sa_ref)
        update(sb_ref, j + 1)

    def quad_trip(t, carry):
        pair(4 * t)
        pair(4 * t + 2)
        return carry
    def oct_trip(t, carry):
        pair(8 * t)
        pair(8 * t + 2)
        pair(8 * t + 4)
        pair(8 * t + 6)
        return carry
    def hex_trip(t, carry):
        for c in range(8):
            pair(16 * t + 2 * c)
        return carry
    n_hexes = i // 16
    lax.fori_loop(0, n_hexes, hex_trip, 0)
    n_octs = i // 8
    lax.fori_loop(2 * n_hexes, n_octs, oct_trip, 0)
    n_quads = i // 4
    lax.fori_loop(2 * n_octs, n_quads, quad_trip, 0)

    def pair_trip(t, carry):
        pair(4 * n_quads + 2 * t)
        return carry
    lax.fori_loop(0, (i - 4 * n_quads + 1) // 2, pair_trip, 0)

    out_t = acc_ref[0:dh, :] / acc_ref[dh:dh + 1, :]
    for g in range(group):
        o_ref[:, g * dh:(g + 1) * dh] = out_t[:, g * blk:(g + 1) * blk].T.astype(o_ref.dtype)


def _moba_attention(q, k, v):
    h, s, dh = q.shape
    hkv = k.shape[0]
    group = h // hkv
    blk = MOBA_BLOCK
    nb = s // blk
    rows = group * blk
    assert nb <= LANES and dh == LANES
    block_of_key = jnp.arange(s, dtype=jnp.int32)[:, None] // blk
    onehot = (block_of_key == jnp.arange(LANES, dtype=jnp.int32)[None, :]).astype(BF16)
    k_aug = jnp.concatenate([k, jnp.broadcast_to(onehot, (hkv, s, LANES))], axis=-1)
    v_t = v.reshape(hkv, nb, blk, dh).transpose(0, 1, 3, 2)
    ones_rows = jnp.zeros((hkv, nb, BF16_SUBLANES, blk), BF16).at[:, :, 0, :].set(1.0)
    vt_aug = jnp.concatenate([v_t, ones_rows], axis=2)
    vrows = dh + BF16_SUBLANES
    return pl.pallas_call(
        _moba_kernel,
        grid=(hkv, nb),
        in_specs=[
            pl.BlockSpec((group, blk, dh), lambda hh, i: (hh, i, 0)),
            pl.BlockSpec((None, s, 2 * LANES), lambda hh, i: (hh, 0, 0)),
            pl.BlockSpec((None, nb, vrows, blk), lambda hh, i: (hh, 0, 0, 0)),
        ],
        out_specs=pl.BlockSpec((blk, group * dh), lambda hh, i: (i, hh)),
        out_shape=jax.ShapeDtypeStruct((s, h * dh), BF16),
        scratch_shapes=[
            pltpu.VMEM((nb, dh), F32),
            pltpu.VMEM((2 * LANES, rows), BF16),
            pltpu.VMEM((blk, rows), F32),
            pltpu.VMEM((blk, rows), F32),
            pltpu.VMEM((1, rows), F32),
            pltpu.VMEM((vrows, rows), F32),
        ],
        compiler_params=_params(
            ("arbitrary", "arbitrary"),
            rows * dh * 2 + s * 2 * LANES * 2 + nb * vrows * blk * 2 + blk * group * dh * 2,
            (2 * blk + vrows + LANES + 1) * rows * 4 + 6 * blk * rows * 4),
        name="moba_attention",
    )(q, k_aug, vt_aug)


def _swa_kernel(q_ref, kp_ref, ko_ref, vp_ref, vo_ref, sink_ref, bias_ref, o_ref, *, nq):
    pairs, span, _ = q_ref.shape
    blk = span // nq
    rows = pairs * blk
    odd = (pl.program_id(0) % 2) == 1
    first_step = pl.program_id(1) == 0

    lane = lax.broadcasted_iota(jnp.int32, (span + blk, LANES), 1)
    lo = lane < B_HEAD_DIM

    def strip(prev_ref, own_ref):
        x = jnp.concatenate([prev_ref[...], own_ref[...]], axis=0)
        swapped = pltpu.roll(x, B_HEAD_DIM, axis=1)
        mine_lo = jnp.where(odd, swapped, x)
        mine_hi = jnp.where(odd, x, swapped)
        return mine_lo, mine_hi

    k_lo, k_hi = strip(kp_ref, ko_ref)
    v_lo, v_hi = strip(vp_ref, vo_ref)
    zero = jnp.zeros_like(k_lo)
    k_both = jnp.where(lo, k_lo, k_hi)
    v_first = jnp.where(lo, v_lo, zero)
    v_second = jnp.where(lo, zero, v_hi)

    qlane = lax.broadcasted_iota(jnp.int32, (rows, LANES), 1)
    q_is_first = qlane < B_HEAD_DIM

    chains = []
    for u in range(nq):
        band = slice(u * blk, (u + 2) * blk)
        q = q_ref[:, u * blk:(u + 1) * blk, :].reshape(rows, LANES)
        qzero = jnp.zeros_like(q)
        bias = bias_ref[1] if u else bias_ref[jnp.where(first_step, 0, 1)]
        for q_half, v_half, sink in (
                (jnp.where(q_is_first, q, qzero), v_first, sink_ref[0:1, :]),
                (jnp.where(q_is_first, qzero, q), v_second, sink_ref[1:2, :])):
            s = lax.dot_general(k_both[band], q_half, _NT, preferred_element_type=F32)
            chains.append((s + bias, v_half[band], sink))
    probs = []
    for s, v_band, sink in chains:
        m = jnp.maximum(jnp.max(s, axis=0, keepdims=True), sink)
        p = jnp.exp2(s - m)
        denom = jnp.sum(p, axis=0, keepdims=True) + jnp.exp2(sink - m)
        probs.append((p.astype(BF16), v_band, denom))
    outs = [lax.dot_general(v_band, p, _TN, preferred_element_type=F32) / denom
            for p, v_band, denom in probs]
    for u in range(nq):
        out_t = outs[2 * u] + outs[2 * u + 1]
        for pr in range(pairs):
            o_ref[u * blk:(u + 1) * blk, pr * LANES:(pr + 1) * LANES] = (
                out_t[:, pr * blk:(pr + 1) * blk].T.astype(o_ref.dtype))


def _swa_attention(q, k, v, sinks):
    npair, s, _ = q.shape
    hkv = k.shape[1] // B_HEAD_DIM
    pairs = npair // hkv
    blk = SWA_BLOCK
    nq = SWA_STEP_BLOCKS
    span = nq * blk
    rows = pairs * blk
    sink_rows = jnp.repeat(
        (sinks.astype(F32) * LOG2_E).reshape(hkv, pairs, 2).transpose(0, 2, 1), blk, axis=-1)
    kk = np.arange(2 * blk)[:, None]
    tt = np.arange(rows)[None, :] % blk
    in_window = (kk > tt) & (kk <= tt + blk)
    band_bias = jnp.asarray(
        np.where(np.stack([in_window & (kk >= blk), in_window]), 0.0, NEG), dtype=F32)
    prev = pl.BlockSpec((blk, LANES), lambda hh, n: (jnp.maximum(n * nq - 1, 0), hh // 2))
    own = pl.BlockSpec((span, LANES), lambda hh, n: (n, hh // 2))
    return pl.pallas_call(
        functools.partial(_swa_kernel, nq=nq),
        grid=(hkv, s // span),
        in_specs=[
            pl.BlockSpec((pairs, span, LANES), lambda hh, n: (hh, n, 0)),
            prev, own, prev, own,
            pl.BlockSpec((None, 2, rows), lambda hh, n: (hh, 0, 0)),
            pl.BlockSpec((2, 2 * blk, rows), lambda hh, n: (0, 0, 0)),
        ],
        out_specs=pl.BlockSpec((span, pairs * LANES), lambda hh, n: (n, hh)),
        out_shape=jax.ShapeDtypeStruct((s, npair * LANES), BF16),
        compiler_params=_params(
            ("parallel", "parallel"),
            2 * pairs * span * LANES * 2 + 2 * (span + blk) * LANES * 2 + 2 * rows * 4
            + 4 * blk * rows * 4,
            12 * blk * rows * 4),
        name="swa_attention",
    )(q, k, k, v, v, sink_rows, band_bias)


def kernel(x, ffn_norm, ffn_w_in, ffn_w_out, attn_norm, a_w_qkv, a_w_o, kv_norm,
           w_kv_shared, b_w_q, b_sinks, b_w_o, final_norm):
    b, s, d = x.shape
    depth = ffn_norm.shape[0]
    n_a = a_w_qkv.shape[0]
    a_q = a_w_o.shape[1]
    a_kv = (a_w_qkv.shape[2] - a_q) // 2
    b_kv = w_kv_shared.shape[1] // 2
    rope_a = _rope_tables(s, A_HEAD_DIM)
    rope_b = _rope_tables(s, B_HEAD_DIM)
    w_in, w_out = ffn_w_in, ffn_w_out
    w_qkv, w_ao = a_w_qkv.astype(BF16), a_w_o.astype(BF16)
    w_kv, w_bq, w_bo = w_kv_shared.astype(BF16), b_w_q.astype(BF16), b_w_o.astype(BF16)

    def gains_entering(layer):
        if layer == depth:
            return []
        return ([kv_norm] if layer == n_a else []) + [ffn_norm[layer, 0]]

    outs = []
    for bi in range(b):
        xs = x.reshape(s, d) if b == 1 else x[bi]
        sumsq, hs = _prenorm(xs, jnp.stack(gains_entering(0)))
        k_sh = v_sh = None
        for layer in range(depth):
            hs = list(hs)
            if layer == n_a:
                h_kv = hs.pop(0)
                k_sh = _proj(h_kv, sumsq, w_kv, 0, b_kv, rope=rope_b, half=B_HEAD_DIM // 2)
                v_sh = _proj(h_kv, sumsq, w_kv, b_kv, b_kv)
            act, w_down = _swiglu_up(hs[0], sumsq, w_in, w_out, (layer, 0))
            xs, sumsq, (h,) = _matmul_residual(act, w_down, (), xs, 0.5, 512, 512,
                                               [attn_norm[layer]])
            if layer < n_a:
                w = w_qkv[layer]
                q = _proj(h, sumsq, w, 0, a_q, rope=rope_a, half=A_HEAD_DIM // 2,
                          scale=A_HEAD_DIM ** -0.5 * LOG2_E, group_major=True)
                k = _proj(h, sumsq, w, a_q, a_kv, rope=rope_a, half=A_HEAD_DIM // 2,
                          group_major=True)
                v = _proj(h, sumsq, w, a_q + a_kv, a_kv, group_major=True)
                o = _moba_attention(q, k, v)
                w_o, lead = w_ao, (layer,)
            else:
                li = layer - n_a
                q = _proj(h, sumsq, w_bq[li], 0, w_bq.shape[2], rope=rope_b,
                          half=B_HEAD_DIM // 2, scale=B_HEAD_DIM ** -0.5 * LOG2_E, group_major=True)
                o = _swa_attention(q, k_sh, v_sh, b_sinks[li])
                w_o, lead = w_bo, (li,)
            xs, sumsq, (h,) = _matmul_residual(o, w_o, lead, xs, 1.0, 1024, 512,
                                               [ffn_norm[layer, 1]])
            act, w_down = _swiglu_up(h, sumsq, w_in, w_out, (layer, 1))
            xs, sumsq, hs = _matmul_residual(act, w_down, (), xs, 0.5, 512, 512,
                                             gains_entering(layer + 1))
        outs.append(_final_norm(xs, sumsq, final_norm))
    return outs[0].reshape(b, s, d) if b == 1 else jnp.stack(outs, axis=0)
```

```python
import functools

import jax
import jax.numpy as jnp
import numpy as np
from jax import lax
from jax.experimental import pallas as pl
from jax.experimental.pallas import tpu as pltpu

F32 = jnp.float32
BF16 = jnp.bfloat16

RMS_EPS = 1e-6
ROPE_THETA = 10000.0
NEG = -1e30
LOG2_E = 1.4426950408889634
TAKEN = -3e38

A_HEAD_DIM = 128
MOBA_BLOCK = 256
MOBA_TOPK = 3

B_HEAD_DIM = 64
SWA_BLOCK = 128
SWA_STEP_BLOCKS = 8

LANES = 128
MXU_COLS = 256
BF16_SUBLANES = 16
UP_TILE_N = 256
DOWN_TILE = 512
V7X_VMEM_CAP = 60000 * 1024

_NT = (((1,), (1,)), ((), ()))
_TN = (((0,), (0,)), ((), ()))


def _params(semantics, block_bytes, scratch_bytes=0):
    need = 2 * block_bytes + scratch_bytes
    return pltpu.CompilerParams(
        dimension_semantics=semantics,
        vmem_limit_bytes=int(min(V7X_VMEM_CAP, max(need, 16 * 1024 * 1024))),
    )


def _lane_tile_sum(x):
    total = x[:, 0:LANES]
    for t in range(1, x.shape[1] // LANES):
        total = total + x[:, t * LANES:(t + 1) * LANES]
    return total


def _row_factor(sumsq_ref, width):
    mean = jnp.sum(sumsq_ref[...], axis=-1, keepdims=True) * (1.0 / width)
    return jnp.broadcast_to(lax.rsqrt(mean + RMS_EPS), sumsq_ref.shape)


def _prenorm_kernel(x_ref, g_ref, sumsq_ref, *o_refs):
    x = x_ref[...]
    sumsq_ref[...] = _lane_tile_sum(x * x)
    for n, o_ref in enumerate(o_refs):
        o_ref[...] = (x * g_ref[n:n + 1, :]).astype(o_ref.dtype)


def _prenorm(x, gains):
    s, d = x.shape
    n = gains.shape[0]
    tm = min(512, s)
    row = pl.BlockSpec((tm, d), lambda i: (i, 0))
    outs = pl.pallas_call(
        _prenorm_kernel,
        grid=(s // tm,),
        in_specs=[row, pl.BlockSpec((n, d), lambda i: (0, 0))],
        out_specs=[pl.BlockSpec((tm, LANES), lambda i: (i, 0))] + [row] * n,
        out_shape=[jax.ShapeDtypeStruct((s, LANES), F32)] + [jax.ShapeDtypeStruct((s, d), BF16)] * n,
        compiler_params=_params(("parallel",), tm * d * (4 + 2 * n), tm * d * 4),
        name="prenorm",
    )(x, gains)
    return outs[0], outs[1:]


def _final_norm_kernel(x_ref, sumsq_ref, g_ref, o_ref):
    mean = jnp.sum(sumsq_ref[...], axis=-1, keepdims=True) * (1.0 / x_ref.shape[1])
    o_ref[...] = x_ref[...] * lax.rsqrt(mean + RMS_EPS) * g_ref[...]


def _final_norm(x, sumsq, gain):
    s, d = x.shape
    tm = min(512, s)
    row = pl.BlockSpec((tm, d), lambda i: (i, 0))
    return pl.pallas_call(
        _final_norm_kernel,
        grid=(s // tm,),
        in_specs=[row, pl.BlockSpec((tm, LANES), lambda i: (i, 0)),
                  pl.BlockSpec((1, d), lambda i: (0, 0))],
        out_specs=row,
        out_shape=jax.ShapeDtypeStruct((s, d), F32),
        compiler_params=_params(("parallel",), tm * d * 8, tm * d * 4),
        name="final_norm",
    )(x, sumsq, gain[None, :])


def _swiglu_up_kernel(h_ref, sumsq_ref, wg_ref, wu_ref, wd_ref, o_ref, wd_bf16_ref, r_ref):
    @pl.when(pl.program_id(1) == 0)
    def _row_factors():
        r_ref[...] = _row_factor(sumsq_ref, h_ref.shape[1])

    wd = wd_ref[...].astype(BF16)
    tile = wd_bf16_ref.shape[2]
    for t in range(wd_bf16_ref.shape[0]):
        wd_bf16_ref[t] = wd[:, t * tile:(t + 1) * tile]

    h = h_ref[...]
    r = r_ref[...]
    g = jnp.dot(h, wg_ref[...].astype(BF16), preferred_element_type=F32)
    u = jnp.dot(h, wu_ref[...].astype(BF16), preferred_element_type=F32)
    for t in range(g.shape[1] // LANES):
        lanes = slice(t * LANES, (t + 1) * LANES)
        gt = g[:, lanes] * r
        o_ref[:, lanes] = (gt * (1.0 / (1.0 + jnp.exp(-gt))) * (u[:, lanes] * r)).astype(o_ref.dtype)


def _swiglu_up(h, sumsq, w_in, w_down, lead, down_tile):
    s, d = h.shape
    f = w_in.shape[-1] // 2
    tm = min(1024, s)
    tn = UP_TILE_N
    nj = f // tn
    squeezed = (None,) * len(lead)
    n_steps = (s // tm) * nj
    share = f // n_steps
    assert share * n_steps == f and share % BF16_SUBLANES == 0, (f, n_steps)
    return pl.pallas_call(
        _swiglu_up_kernel,
        grid=(s // tm, nj),
        in_specs=[
            pl.BlockSpec((tm, d), lambda i, j: (i, 0)),
            pl.BlockSpec((tm, LANES), lambda i, j: (i, 0)),
            pl.BlockSpec(squeezed + (d, tn), lambda i, j: lead + (0, j)),
            pl.BlockSpec(squeezed + (d, tn), lambda i, j: lead + (0, j + nj)),
            pl.BlockSpec(squeezed + (share, d), lambda i, j: lead + (i * nj + j, 0)),
        ],
        out_specs=[pl.BlockSpec((tm, tn), lambda i, j: (i, j)),
                   pl.BlockSpec((d // down_tile, share, down_tile), lambda i, j: (0, i * nj + j, 0))],
        out_shape=[jax.ShapeDtypeStruct((s, f), BF16),
                   jax.ShapeDtypeStruct((d // down_tile, f, down_tile), BF16)],
        scratch_shapes=[pltpu.VMEM((tm, LANES), F32)],
        compiler_params=_params(
            ("parallel", "arbitrary"),
            tm * d * 2 + 2 * d * tn * w_in.dtype.itemsize + tm * tn * 2 + share * d * 6,
            4 * tm * tn * 4 + 2 * d * tn * 2),
        name="swiglu_up",
    )(h, sumsq, w_in, w_in, w_down)


def _matmul_residual_kernel(a_ref, w_ref, r_ref, *refs, alpha, n_gains):
    if n_gains:
        g_ref, o_ref, sumsq_ref, *h_refs = refs
    else:
        o_ref, sumsq_ref = refs
        h_refs = []
    acc = jnp.dot(a_ref[...], w_ref[...].astype(BF16), preferred_element_type=F32)
    x_new = r_ref[...] + alpha * acc
    o_ref[...] = x_new
    part = _lane_tile_sum(x_new * x_new)

    @pl.when(pl.program_id(1) == 0)
    def _first():
        sumsq_ref[...] = part

    @pl.when(pl.program_id(1) != 0)
    def _rest():
        sumsq_ref[...] += part

    for n, h_ref in enumerate(h_refs):
        h_ref[...] = (x_new * g_ref[n:n + 1, :]).astype(h_ref.dtype)


def _matmul_residual(a, w, lead, res, alpha, tm, tn, next_gains, tile_major=False):
    s, k = a.shape
    n = res.shape[1]
    ng = len(next_gains)
    tm, tn = min(tm, s), min(tn, n)
    squeezed = (None,) * len(lead)
    tile = pl.BlockSpec((tm, tn), lambda i, j: (i, j))
    if tile_major:
        assert w.shape == (n // tn, k, tn), w.shape
        w_spec = pl.BlockSpec((None, k, tn), lambda i, j: (j, 0, 0))
    else:
        w_spec = pl.BlockSpec(squeezed + (k, tn), lambda i, j: lead + (0, j))
    in_specs = [pl.BlockSpec((tm, k), lambda i, j: (i, 0)), w_spec, tile]
    args = [a, w, res]
    if ng:
        in_specs.append(pl.BlockSpec((ng, tn), lambda i, j: (0, j)))
        args.append(jnp.stack(next_gains))
    outs = pl.pallas_call(
        functools.partial(_matmul_residual_kernel, alpha=alpha, n_gains=ng),
        grid=(s // tm, n // tn),
        in_specs=in_specs,
        out_specs=[tile, pl.BlockSpec((tm, LANES), lambda i, j: (i, 0))] + [tile] * ng,
        out_shape=[jax.ShapeDtypeStruct((s, n), F32), jax.ShapeDtypeStruct((s, LANES), F32)]
        + [jax.ShapeDtypeStruct((s, n), BF16)] * ng,
        compiler_params=_params(
            ("parallel", "arbitrary"),
            tm * k * 2 + k * tn * w.dtype.itemsize + 2 * tm * tn * 4 + tm * LANES * 4
            + ng * tm * tn * 2,
            3 * tm * tn * 4 + k * tn * 2),
        name="matmul_residual",
    )(*args)
    return outs[0], outs[1], outs[2:]


def _rope_tables(seq, dim):
    half = dim // 2
    inv = 1.0 / (ROPE_THETA ** (jnp.arange(0, dim, 2, dtype=F32) / dim))
    ang = jnp.arange(seq, dtype=F32)[:, None] * inv[None, :]
    cos, sin = jnp.cos(ang), jnp.sin(ang)
    zero = jnp.zeros_like(sin)
    reps = LANES // dim
    cos_t = jnp.tile(jnp.concatenate([cos, cos], axis=-1), (1, reps))
    s_lo = jnp.tile(jnp.concatenate([-sin, zero], axis=-1), (1, reps))
    s_hi = jnp.tile(jnp.concatenate([zero, sin], axis=-1), (1, reps))
    return cos_t, s_lo, s_hi


def _proj_kernel(h_ref, sumsq_ref, w_ref, *refs, half, scale, group_major):
    acc = jnp.dot(h_ref[...], w_ref[...].astype(BF16), preferred_element_type=F32)
    if half:
        cos_ref, slo_ref, shi_ref, o_ref = refs
        cos, s_lo, s_hi = cos_ref[...], slo_ref[...], shi_ref[...]
        one_roll = 2 * half == LANES
        if one_roll:
            s_both = s_lo + s_hi
    else:
        (o_ref,) = refs
    r = _row_factor(sumsq_ref, h_ref.shape[1]) * scale
    for g in range(acc.shape[1] // LANES):
        x = acc[:, g * LANES:(g + 1) * LANES] * r
        if half and one_roll:
            x = x * cos + pltpu.roll(x, half, axis=1) * s_both
        elif half:
            x = (x * cos + pltpu.roll(x, LANES - half, axis=1) * s_lo
                 + pltpu.roll(x, half, axis=1) * s_hi)
        if group_major:
            o_ref[g] = x.astype(o_ref.dtype)
        else:
            o_ref[:, g * LANES:(g + 1) * LANES] = x.astype(o_ref.dtype)


def _proj(h, sumsq, w, col0, ncols, *, rope=None, half=0, scale=1.0, group_major=False):
    s, k = h.shape
    tm = min(1024, s)
    tn = min(512, ncols)
    j0 = col0 // tn
    in_specs = [
        pl.BlockSpec((tm, k), lambda i, j: (i, 0)),
        pl.BlockSpec((tm, LANES), lambda i, j: (i, 0)),
        pl.BlockSpec((k, tn), lambda i, j: (0, j + j0)),
    ]
    args = [h, sumsq, w]
    if half:
        tab = pl.BlockSpec((tm, LANES), lambda i, j: (i, 0))
        in_specs += [tab, tab, tab]
        args += list(rope)
    if group_major:
        gpt = tn // LANES
        out_spec = pl.BlockSpec((gpt, tm, LANES), lambda i, j: (j, i, 0))
        out_shape = jax.ShapeDtypeStruct((ncols // LANES, s, LANES), BF16)
    else:
        out_spec = pl.BlockSpec((tm, tn), lambda i, j: (i, j))
        out_shape = jax.ShapeDtypeStruct((s, ncols), BF16)
    return pl.pallas_call(
        functools.partial(_proj_kernel, half=half, scale=scale, group_major=group_major),
        grid=(s // tm, ncols // tn),
        in_specs=in_specs,
        out_specs=out_spec,
        out_shape=out_shape,
        compiler_params=_params(
            ("parallel", "arbitrary"),
            tm * k * 2 + k * tn * w.dtype.itemsize + 4 * tm * LANES * 4 + tm * tn * 2,
            3 * tm * tn * 4 + k * tn * 2),
        name="proj",
    )(*args)


def _moba_kernel(q_ref, k_ref, vt_ref, o_ref, kmean_ref, qa_ref, sa_ref, sb_ref, m_ref, acc_ref):
    group, blk, dh = q_ref.shape
    nb = vt_ref.shape[0]
    rows = group * blk
    i = pl.program_id(1)

    @pl.when(i == 0)
    def _block_means():
        def body(j, carry):
            kj = k_ref[pl.ds(pl.multiple_of(j * blk, blk), blk), 0:dh].astype(F32)
            kmean_ref[pl.ds(j, 1), :] = jnp.sum(kj, axis=0, keepdims=True) * (1.0 / blk)
            return carry
        lax.fori_loop(0, nb, body, 0)

    q_t = q_ref[...].reshape(rows, dh).astype(F32).T.astype(BF16)
    qa_ref[0:dh, :] = q_t

    own = pl.multiple_of(i * blk, blk)
    s_own = jnp.dot(k_ref[pl.ds(own, blk), 0:dh], q_t, preferred_element_type=F32)

    km = kmean_ref[...]
    km_hi = km.astype(BF16)
    km_lo = (km - km_hi.astype(F32)).astype(BF16)
    gate = (jnp.dot(km_hi, q_t, preferred_element_type=F32)
            + jnp.dot(km_lo, q_t, preferred_element_type=F32))
    bidx = lax.broadcasted_iota(jnp.int32, (nb, rows), 0)
    past = bidx < i
    bidx = bidx.astype(F32)
    gate = jnp.where(past, gate, NEG)
    bias = jnp.full((nb, rows), NEG, F32)
    for _ in range(min(MOBA_TOPK, nb)):
        top = jnp.max(gate, axis=0, keepdims=True)
        first = jnp.min(jnp.where(gate == top, bidx, float(nb)), axis=0, keepdims=True)
        pick = bidx == first
        bias = jnp.where(pick & past, 0.0, bias)
        gate = jnp.where(pick, TAKEN, gate)
    qa_ref[dh:dh + nb, :] = bias.astype(BF16)
    if nb < LANES:
        qa_ref[dh + nb:dh + LANES, :] = jnp.zeros((LANES - nb, rows), BF16)

    s = s_own
    kpos = lax.broadcasted_iota(jnp.int32, (blk, rows), 0)
    qpos = lax.broadcasted_iota(jnp.int32, (blk, rows), 1) & (blk - 1)
    s = jnp.where(kpos <= qpos, s, NEG)
    m = jnp.max(s, axis=0, keepdims=True)
    m_ref[...] = m
    acc_ref[...] = jnp.dot(vt_ref[i], jnp.exp2(s - m).astype(BF16), preferred_element_type=F32)

    def scores(j, s_ref):
        off = pl.multiple_of(jnp.minimum(j, nb - 1) * blk, blk)
        s_ref[...] = jnp.dot(k_ref[pl.ds(off, blk), :], qa_ref[...], preferred_element_type=F32)

    reread = pl.multiple_of(jnp.minimum(i, 0) * blk, blk)

    def update(s_ref, j):
        v_t = vt_ref[jnp.minimum(j, nb - 1)]
        for c in range(rows // MXU_COLS):
            cols = slice(c * MXU_COLS, (c + 1) * MXU_COLS)
            s = s_ref[pl.ds(reread, blk), cols]
            m_old = m_ref[:, cols]
            m_new = jnp.maximum(m_old, jnp.max(s, axis=0, keepdims=True))
            m_ref[:, cols] = m_new
            p = jnp.exp2(s - m_new).astype(BF16)
            acc_ref[:, cols] = jnp.exp2(m_old - m_new) * acc_ref[:, cols] + jnp.dot(
                v_t, p, preferred_element_type=F32)

    scores(0, sa_ref)

    def pair(j):
        scores(j + 1, sb_ref)
        update(sa_ref, j)
        scores(j + 2, sa_ref)
        update(sb_ref, j + 1)

    def quad_trip(t, carry):
        pair(4 * t)
        pair(4 * t + 2)
        return carry
    def oct_trip(t, carry):
        pair(8 * t)
        pair(8 * t + 2)
        pair(8 * t + 4)
        pair(8 * t + 6)
        return carry
    def hex_trip(t, carry):
        for c in range(8):
            pair(16 * t + 2 * c)
        return carry
    n_hexes = i // 16
    lax.fori_loop(0, n_hexes, hex_trip, 0)
    n_octs = i // 8
    lax.fori_loop(2 * n_hexes, n_octs, oct_trip, 0)
    n_quads = i // 4
    lax.fori_loop(2 * n_octs, n_quads, quad_trip, 0)

    def pair_trip(t, carry):
        pair(4 * n_quads + 2 * t)
        return carry
    lax.fori_loop(0, (i - 4 * n_quads + 1) // 2, pair_trip, 0)

    out_t = acc_ref[0:dh, :] / acc_ref[dh:dh + 1, :]
    for g in range(group):
        o_ref[:, g * dh:(g + 1) * dh] = out_t[:, g * blk:(g + 1) * blk].T.astype(o_ref.dtype)


def _moba_attention(q, k, v):
    h, s, dh = q.shape
    hkv = k.shape[0]
    group = h // hkv
    blk = MOBA_BLOCK
    nb = s // blk
    rows = group * blk
    assert nb <= LANES and dh == LANES
    block_of_key = jnp.arange(s, dtype=jnp.int32)[:, None] // blk
    onehot = (block_of_key == jnp.arange(LANES, dtype=jnp.int32)[None, :]).astype(BF16)
    k_aug = jnp.concatenate([k, jnp.broadcast_to(onehot, (hkv, s, LANES))], axis=-1)
    v_t = v.reshape(hkv, nb, blk, dh).transpose(0, 1, 3, 2)
    ones_rows = jnp.zeros((hkv, nb, BF16_SUBLANES, blk), BF16).at[:, :, 0, :].set(1.0)
    vt_aug = jnp.concatenate([v_t, ones_rows], axis=2)
    vrows = dh + BF16_SUBLANES
    return pl.pallas_call(
        _moba_kernel,
        grid=(hkv, nb),
        in_specs=[
            pl.BlockSpec((group, blk, dh), lambda hh, i: (hh, i, 0)),
            pl.BlockSpec((None, s, 2 * LANES), lambda hh, i: (hh, 0, 0)),
            pl.BlockSpec((None, nb, vrows, blk), lambda hh, i: (hh, 0, 0, 0)),
        ],
        out_specs=pl.BlockSpec((blk, group * dh), lambda hh, i: (i, hh)),
        out_shape=jax.ShapeDtypeStruct((s, h * dh), BF16),
        scratch_shapes=[
            pltpu.VMEM((nb, dh), F32),
            pltpu.VMEM((2 * LANES, rows), BF16),
            pltpu.VMEM((blk, rows), F32),
            pltpu.VMEM((blk, rows), F32),
            pltpu.VMEM((1, rows), F32),
            pltpu.VMEM((vrows, rows), F32),
        ],
        compiler_params=_params(
            ("arbitrary", "arbitrary"),
            rows * dh * 2 + s * 2 * LANES * 2 + nb * vrows * blk * 2 + blk * group * dh * 2,
            (2 * blk + vrows + LANES + 1) * rows * 4 + 6 * blk * rows * 4),
        name="moba_attention",
    )(q, k_aug, vt_aug)


def _swa_kernel(q_ref, kp_ref, ko_ref, vp_ref, vo_ref, sink_ref, bias_ref, o_ref, *, nq):
    pairs, span, _ = q_ref.shape
    blk = span // nq
    rows = pairs * blk
    odd = (pl.program_id(0) % 2) == 1
    first_step = pl.program_id(1) == 0

    lane = lax.broadcasted_iota(jnp.int32, (span + blk, LANES), 1)
    lo = lane < B_HEAD_DIM

    def strip(prev_ref, own_ref):
        x = jnp.concatenate([prev_ref[...], own_ref[...]], axis=0)
        swapped = pltpu.roll(x, B_HEAD_DIM, axis=1)
        mine_lo = jnp.where(odd, swapped, x)
        mine_hi = jnp.where(odd, x, swapped)
        return mine_lo, mine_hi

    k_lo, k_hi = strip(kp_ref, ko_ref)
    v_lo, v_hi = strip(vp_ref, vo_ref)
    zero = jnp.zeros_like(k_lo)
    k_both = jnp.where(lo, k_lo, k_hi)
    v_first = jnp.where(lo, v_lo, zero)
    v_second = jnp.where(lo, zero, v_hi)

    qlane = lax.broadcasted_iota(jnp.int32, (rows, LANES), 1)
    q_is_first = qlane < B_HEAD_DIM

    chains = []
    for u in range(nq):
        band = slice(u * blk, (u + 2) * blk)
        q = q_ref[:, u * blk:(u + 1) * blk, :].reshape(rows, LANES)
        qzero = jnp.zeros_like(q)
        bias = bias_ref[1] if u else bias_ref[jnp.where(first_step, 0, 1)]
        for q_half, v_half, sink in (
                (jnp.where(q_is_first, q, qzero), v_first, sink_ref[0:1, :]),
                (jnp.where(q_is_first, qzero, q), v_second, sink_ref[1:2, :])):
            s = lax.dot_general(k_both[band], q_half, _NT, preferred_element_type=F32)
            chains.append((s + bias, v_half[band], sink))
    probs = []
    for s, v_band, sink in chains:
        m = jnp.maximum(jnp.max(s, axis=0, keepdims=True), sink)
        p = jnp.exp2(s - m)
        denom = jnp.sum(p, axis=0, keepdims=True) + jnp.exp2(sink - m)
        probs.append((p.astype(BF16), v_band, denom))
    outs = [lax.dot_general(v_band, p, _TN, preferred_element_type=F32) / denom
            for p, v_band, denom in probs]
    for u in range(nq):
        out_t = outs[2 * u] + outs[2 * u + 1]
        for pr in range(pairs):
            o_ref[u * blk:(u + 1) * blk, pr * LANES:(pr + 1) * LANES] = (
                out_t[:, pr * blk:(pr + 1) * blk].T.astype(o_ref.dtype))


def _swa_attention(q, k, v, sinks):
    npair, s, _ = q.shape
    hkv = k.shape[1] // B_HEAD_DIM
    pairs = npair // hkv
    blk = SWA_BLOCK
    nq = SWA_STEP_BLOCKS
    span = nq * blk
    rows = pairs * blk
    sink_rows = jnp.repeat(
        (sinks.astype(F32) * LOG2_E).reshape(hkv, pairs, 2).transpose(0, 2, 1), blk, axis=-1)
    kk = np.arange(2 * blk)[:, None]
    tt = np.arange(rows)[None, :] % blk
    in_window = (kk > tt) & (kk <= tt + blk)
    band_bias = jnp.asarray(
        np.where(np.stack([in_window & (kk >= blk), in_window]), 0.0, NEG), dtype=F32)
    prev = pl.BlockSpec((blk, LANES), lambda hh, n: (jnp.maximum(n * nq - 1, 0), hh // 2))
    own = pl.BlockSpec((span, LANES), lambda hh, n: (n, hh // 2))
    return pl.pallas_call(
        functools.partial(_swa_kernel, nq=nq),
        grid=(hkv, s // span),
        in_specs=[
            pl.BlockSpec((pairs, span, LANES), lambda hh, n: (hh, n, 0)),
            prev, own, prev, own,
            pl.BlockSpec((None, 2, rows), lambda hh, n: (hh, 0, 0)),
            pl.BlockSpec((2, 2 * blk, rows), lambda hh, n: (0, 0, 0)),
        ],
        out_specs=pl.BlockSpec((span, pairs * LANES), lambda hh, n: (n, hh)),
        out_shape=jax.ShapeDtypeStruct((s, npair * LANES), BF16),
        compiler_params=_params(
            ("parallel", "parallel"),
            2 * pairs * span * LANES * 2 + 2 * (span + blk) * LANES * 2 + 2 * rows * 4
            + 4 * blk * rows * 4,
            12 * blk * rows * 4),
        name="swa_attention",
    )(q, k, k, v, v, sink_rows, band_bias)


def kernel(x, ffn_norm, ffn_w_in, ffn_w_out, attn_norm, a_w_qkv, a_w_o, kv_norm,
           w_kv_shared, b_w_q, b_sinks, b_w_o, final_norm):
    b, s, d = x.shape
    depth = ffn_norm.shape[0]
    n_a = a_w_qkv.shape[0]
    a_q = a_w_o.shape[1]
    a_kv = (a_w_qkv.shape[2] - a_q) // 2
    b_kv = w_kv_shared.shape[1] // 2
    rope_a = _rope_tables(s, A_HEAD_DIM)
    rope_b = _rope_tables(s, B_HEAD_DIM)
    w_in, w_out = ffn_w_in, ffn_w_out
    w_qkv, w_ao = a_w_qkv.astype(BF16), a_w_o.astype(BF16)
    w_kv, w_bq, w_bo = w_kv_shared.astype(BF16), b_w_q.astype(BF16), b_w_o.astype(BF16)

    def gains_entering(layer):
        if layer == depth:
            return []
        return ([kv_norm] if layer == n_a else []) + [ffn_norm[layer, 0]]

    outs = []
    for bi in range(b):
        xs = x.reshape(s, d) if b == 1 else x[bi]
        sumsq, hs = _prenorm(xs, jnp.stack(gains_entering(0)))
        k_sh = v_sh = None
        for layer in range(depth):
            hs = list(hs)
            if layer == n_a:
                h_kv = hs.pop(0)
                k_sh = _proj(h_kv, sumsq, w_kv, 0, b_kv, rope=rope_b, half=B_HEAD_DIM // 2)
                v_sh = _proj(h_kv, sumsq, w_kv, b_kv, b_kv)
            act, w_down = _swiglu_up(hs[0], sumsq, w_in, w_out, (layer, 0), DOWN_TILE)
            xs, sumsq, (h,) = _matmul_residual(act, w_down, (), xs, 0.5, DOWN_TILE, DOWN_TILE,
                                               [attn_norm[layer]], tile_major=True)
            if layer < n_a:
                w = w_qkv[layer]
                q = _proj(h, sumsq, w, 0, a_q, rope=rope_a, half=A_HEAD_DIM // 2,
                          scale=A_HEAD_DIM ** -0.5 * LOG2_E, group_major=True)
                k = _proj(h, sumsq, w, a_q, a_kv, rope=rope_a, half=A_HEAD_DIM // 2,
                          group_major=True)
                v = _proj(h, sumsq, w, a_q + a_kv, a_kv, group_major=True)
                o = _moba_attention(q, k, v)
                w_o, lead = w_ao, (layer,)
            else:
                li = layer - n_a
                q = _proj(h, sumsq, w_bq[li], 0, w_bq.shape[2], rope=rope_b,
                          half=B_HEAD_DIM // 2, scale=B_HEAD_DIM ** -0.5 * LOG2_E, group_major=True)
                o = _swa_attention(q, k_sh, v_sh, b_sinks[li])
                w_o, lead = w_bo, (li,)
            xs, sumsq, (h,) = _matmul_residual(o, w_o, lead, xs, 1.0, 1024, 512,
                                               [ffn_norm[layer, 1]])
            act, w_down = _swiglu_up(h, sumsq, w_in, w_out, (layer, 1), DOWN_TILE)
            xs, sumsq, hs = _matmul_residual(act, w_down, (), xs, 0.5, DOWN_TILE, DOWN_TILE,
                                             gains_entering(layer + 1), tile_major=True)
        outs.append(_final_norm(xs, sumsq, final_norm))
    return outs[0].reshape(b, s, d) if b == 1 else jnp.stack(outs, axis=0)
```

```python
import functools

import jax
import jax.numpy as jnp
import numpy as np
from jax import lax
from jax.experimental import pallas as pl
from jax.experimental.pallas import tpu as pltpu

F32 = jnp.float32
BF16 = jnp.bfloat16

RMS_EPS = 1e-6
ROPE_THETA = 10000.0
NEG = -1e30
LOG2_E = 1.4426950408889634
TAKEN = -3e38

A_HEAD_DIM = 128
MOBA_BLOCK = 256
MOBA_TOPK = 3

B_HEAD_DIM = 64
SWA_BLOCK = 128
SWA_STEP_BLOCKS = 8

LANES = 128
MXU_COLS = 256
BF16_SUBLANES = 16
UP_TILE_N = 256
V7X_VMEM_CAP = 60000 * 1024

_NT = (((1,), (1,)), ((), ()))
_TN = (((0,), (0,)), ((), ()))


def _params(semantics, block_bytes, scratch_bytes=0, fuse_inputs=None):
    need = 2 * block_bytes + scratch_bytes
    return pltpu.CompilerParams(
        dimension_semantics=semantics,
        vmem_limit_bytes=int(min(V7X_VMEM_CAP, max(need, 16 * 1024 * 1024))),
        allow_input_fusion=fuse_inputs,
    )


def _lane_tile_sum(x):
    total = x[:, 0:LANES]
    for t in range(1, x.shape[1] // LANES):
        total = total + x[:, t * LANES:(t + 1) * LANES]
    return total


def _row_factor(sumsq_ref, width):
    mean = jnp.sum(sumsq_ref[...], axis=-1, keepdims=True) * (1.0 / width)
    return jnp.broadcast_to(lax.rsqrt(mean + RMS_EPS), sumsq_ref.shape)


def _prenorm_kernel(x_ref, g_ref, sumsq_ref, *o_refs):
    x = x_ref[...]
    sumsq_ref[...] = _lane_tile_sum(x * x)
    for n, o_ref in enumerate(o_refs):
        o_ref[...] = (x * g_ref[n:n + 1, :]).astype(o_ref.dtype)


def _prenorm(x, gains):
    s, d = x.shape
    n = gains.shape[0]
    tm = min(512, s)
    row = pl.BlockSpec((tm, d), lambda i: (i, 0))
    outs = pl.pallas_call(
        _prenorm_kernel,
        grid=(s // tm,),
        in_specs=[row, pl.BlockSpec((n, d), lambda i: (0, 0))],
        out_specs=[pl.BlockSpec((tm, LANES), lambda i: (i, 0))] + [row] * n,
        out_shape=[jax.ShapeDtypeStruct((s, LANES), F32)] + [jax.ShapeDtypeStruct((s, d), BF16)] * n,
        compiler_params=_params(("parallel",), tm * d * (4 + 2 * n), tm * d * 4),
        name="prenorm",
    )(x, gains)
    return outs[0], outs[1:]


def _final_norm_kernel(x_ref, sumsq_ref, g_ref, o_ref):
    mean = jnp.sum(sumsq_ref[...], axis=-1, keepdims=True) * (1.0 / x_ref.shape[1])
    o_ref[...] = x_ref[...] * lax.rsqrt(mean + RMS_EPS) * g_ref[...]


def _final_norm(x, sumsq, gain):
    s, d = x.shape
    tm = min(512, s)
    row = pl.BlockSpec((tm, d), lambda i: (i, 0))
    return pl.pallas_call(
        _final_norm_kernel,
        grid=(s // tm,),
        in_specs=[row, pl.BlockSpec((tm, LANES), lambda i: (i, 0)),
                  pl.BlockSpec((1, d), lambda i: (0, 0))],
        out_specs=row,
        out_shape=jax.ShapeDtypeStruct((s, d), F32),
        compiler_params=_params(("parallel",), tm * d * 8, tm * d * 4),
        name="final_norm",
    )(x, sumsq, gain[None, :])


def _swiglu_up_kernel(h_ref, sumsq_ref, wg_ref, wu_ref, wd_ref, o_ref, wd_bf16_ref, r_ref):
    @pl.when(pl.program_id(1) == 0)
    def _row_factors():
        r_ref[...] = _row_factor(sumsq_ref, h_ref.shape[1])

    wd_bf16_ref[...] = wd_ref[...].astype(BF16)

    h = h_ref[...]
    r = r_ref[...]
    g = jnp.dot(h, wg_ref[...].astype(BF16), preferred_element_type=F32)
    u = jnp.dot(h, wu_ref[...].astype(BF16), preferred_element_type=F32)
    for t in range(g.shape[1] // LANES):
        lanes = slice(t * LANES, (t + 1) * LANES)
        gt = g[:, lanes] * r
        o_ref[:, lanes] = (gt * (1.0 / (1.0 + jnp.exp(-gt))) * (u[:, lanes] * r)).astype(o_ref.dtype)


def _swiglu_up(h, sumsq, w_in, w_down, lead):
    s, d = h.shape
    f = w_in.shape[-1] // 2
    tm = min(1024, s)
    tn = UP_TILE_N
    nj = f // tn
    squeezed = (None,) * len(lead)
    n_steps = (s // tm) * nj
    share = f // n_steps
    assert share * n_steps == f and share % BF16_SUBLANES == 0, (f, n_steps)
    return pl.pallas_call(
        _swiglu_up_kernel,
        grid=(s // tm, nj),
        in_specs=[
            pl.BlockSpec((tm, d), lambda i, j: (i, 0)),
            pl.BlockSpec((tm, LANES), lambda i, j: (i, 0)),
            pl.BlockSpec(squeezed + (d, tn), lambda i, j: lead + (0, j)),
            pl.BlockSpec(squeezed + (d, tn), lambda i, j: lead + (0, j + nj)),
            pl.BlockSpec(squeezed + (share, d), lambda i, j: lead + (i * nj + j, 0)),
        ],
        out_specs=[pl.BlockSpec((tm, tn), lambda i, j: (i, j)),
                   pl.BlockSpec((share, d), lambda i, j: (i * nj + j, 0))],
        out_shape=[jax.ShapeDtypeStruct((s, f), BF16), jax.ShapeDtypeStruct((f, d), BF16)],
        scratch_shapes=[pltpu.VMEM((tm, LANES), F32)],
        compiler_params=_params(
            ("parallel", "arbitrary"),
            tm * d * 2 + 2 * d * tn * w_in.dtype.itemsize + tm * tn * 2 + share * d * 6,
            4 * tm * tn * 4 + 2 * d * tn * 2),
        name="swiglu_up",
    )(h, sumsq, w_in, w_in, w_down)


def _matmul_residual_kernel(a_ref, w_ref, r_ref, *refs, alpha, n_gains):
    if n_gains:
        g_ref, o_ref, sumsq_ref, *h_refs = refs
    else:
        o_ref, sumsq_ref = refs
        h_refs = []
    acc = jnp.dot(a_ref[...], w_ref[...].astype(BF16), preferred_element_type=F32)
    x_new = r_ref[...] + alpha * acc
    o_ref[...] = x_new
    part = _lane_tile_sum(x_new * x_new)

    @pl.when(pl.program_id(1) == 0)
    def _first():
        sumsq_ref[...] = part

    @pl.when(pl.program_id(1) != 0)
    def _rest():
        sumsq_ref[...] += part

    for n, h_ref in enumerate(h_refs):
        h_ref[...] = (x_new * g_ref[n:n + 1, :]).astype(h_ref.dtype)


def _matmul_residual(a, w, lead, res, alpha, tm, tn, next_gains):
    s, k = a.shape
    n = w.shape[-1]
    ng = len(next_gains)
    tm, tn = min(tm, s), min(tn, n)
    squeezed = (None,) * len(lead)
    tile = pl.BlockSpec((tm, tn), lambda i, j: (i, j))
    in_specs = [
        pl.BlockSpec((tm, k), lambda i, j: (i, 0)),
        pl.BlockSpec(squeezed + (k, tn), lambda i, j: lead + (0, j)),
        tile,
    ]
    args = [a, w, res]
    if ng:
        in_specs.append(pl.BlockSpec((ng, tn), lambda i, j: (0, j)))
        args.append(jnp.stack(next_gains))
    outs = pl.pallas_call(
        functools.partial(_matmul_residual_kernel, alpha=alpha, n_gains=ng),
        grid=(s // tm, n // tn),
        in_specs=in_specs,
        out_specs=[tile, pl.BlockSpec((tm, LANES), lambda i, j: (i, 0))] + [tile] * ng,
        out_shape=[jax.ShapeDtypeStruct((s, n), F32), jax.ShapeDtypeStruct((s, LANES), F32)]
        + [jax.ShapeDtypeStruct((s, n), BF16)] * ng,
        compiler_params=_params(
            ("parallel", "arbitrary"),
            tm * k * 2 + k * tn * w.dtype.itemsize + 2 * tm * tn * 4 + tm * LANES * 4
            + ng * tm * tn * 2,
            3 * tm * tn * 4 + k * tn * 2,
            fuse_inputs=[n == 1 for n in range(len(args))]),
        name="matmul_residual",
    )(*args)
    return outs[0], outs[1], outs[2:]


def _rope_tables(seq, dim):
    half = dim // 2
    inv = 1.0 / (ROPE_THETA ** (jnp.arange(0, dim, 2, dtype=F32) / dim))
    ang = jnp.arange(seq, dtype=F32)[:, None] * inv[None, :]
    cos, sin = jnp.cos(ang), jnp.sin(ang)
    zero = jnp.zeros_like(sin)
    reps = LANES // dim
    cos_t = jnp.tile(jnp.concatenate([cos, cos], axis=-1), (1, reps))
    s_lo = jnp.tile(jnp.concatenate([-sin, zero], axis=-1), (1, reps))
    s_hi = jnp.tile(jnp.concatenate([zero, sin], axis=-1), (1, reps))
    return cos_t, s_lo, s_hi


def _proj_kernel(h_ref, sumsq_ref, w_ref, *refs, half, scale, group_major):
    acc = jnp.dot(h_ref[...], w_ref[...].astype(BF16), preferred_element_type=F32)
    if half:
        cos_ref, slo_ref, shi_ref, o_ref = refs
        cos, s_lo, s_hi = cos_ref[...], slo_ref[...], shi_ref[...]
        one_roll = 2 * half == LANES
        if one_roll:
            s_both = s_lo + s_hi
    else:
        (o_ref,) = refs
    r = _row_factor(sumsq_ref, h_ref.shape[1]) * scale
    for g in range(acc.shape[1] // LANES):
        x = acc[:, g * LANES:(g + 1) * LANES] * r
        if half and one_roll:
            x = x * cos + pltpu.roll(x, half, axis=1) * s_both
        elif half:
            x = (x * cos + pltpu.roll(x, LANES - half, axis=1) * s_lo
                 + pltpu.roll(x, half, axis=1) * s_hi)
        if group_major:
            o_ref[g] = x.astype(o_ref.dtype)
        else:
            o_ref[:, g * LANES:(g + 1) * LANES] = x.astype(o_ref.dtype)


def _proj(h, sumsq, w, col0, ncols, *, rope=None, half=0, scale=1.0, group_major=False):
    s, k = h.shape
    tm = min(1024, s)
    tn = min(512, ncols)
    j0 = col0 // tn
    in_specs = [
        pl.BlockSpec((tm, k), lambda i, j: (i, 0)),
        pl.BlockSpec((tm, LANES), lambda i, j: (i, 0)),
        pl.BlockSpec((k, tn), lambda i, j: (0, j + j0)),
    ]
    args = [h, sumsq, w]
    if half:
        tab = pl.BlockSpec((tm, LANES), lambda i, j: (i, 0))
        in_specs += [tab, tab, tab]
        args += list(rope)
    if group_major:
        gpt = tn // LANES
        out_spec = pl.BlockSpec((gpt, tm, LANES), lambda i, j: (j, i, 0))
        out_shape = jax.ShapeDtypeStruct((ncols // LANES, s, LANES), BF16)
    else:
        out_spec = pl.BlockSpec((tm, tn), lambda i, j: (i, j))
        out_shape = jax.ShapeDtypeStruct((s, ncols), BF16)
    return pl.pallas_call(
        functools.partial(_proj_kernel, half=half, scale=scale, group_major=group_major),
        grid=(s // tm, ncols // tn),
        in_specs=in_specs,
        out_specs=out_spec,
        out_shape=out_shape,
        compiler_params=_params(
            ("parallel", "arbitrary"),
            tm * k * 2 + k * tn * w.dtype.itemsize + 4 * tm * LANES * 4 + tm * tn * 2,
            3 * tm * tn * 4 + k * tn * 2,
            fuse_inputs=[n == 2 for n in range(len(args))]),
        name="proj",
    )(*args)


def _moba_kernel(q_ref, k_ref, vt_ref, o_ref, kmean_ref, qa_ref, sa_ref, sb_ref, m_ref, acc_ref):
    group, blk, dh = q_ref.shape
    nb = vt_ref.shape[0]
    rows = group * blk
    i = pl.program_id(1)

    @pl.when(i == 0)
    def _block_means():
        def body(j, carry):
            kj = k_ref[pl.ds(pl.multiple_of(j * blk, blk), blk), 0:dh].astype(F32)
            kmean_ref[pl.ds(j, 1), :] = jnp.sum(kj, axis=0, keepdims=True) * (1.0 / blk)
            return carry
        lax.fori_loop(0, nb, body, 0)

    q_t = q_ref[...].reshape(rows, dh).astype(F32).T.astype(BF16)
    qa_ref[0:dh, :] = q_t

    own = pl.multiple_of(i * blk, blk)
    s_own = jnp.dot(k_ref[pl.ds(own, blk), 0:dh], q_t, preferred_element_type=F32)

    km = kmean_ref[...]
    km_hi = km.astype(BF16)
    km_lo = (km - km_hi.astype(F32)).astype(BF16)
    gate = (jnp.dot(km_hi, q_t, preferred_element_type=F32)
            + jnp.dot(km_lo, q_t, preferred_element_type=F32))
    bidx = lax.broadcasted_iota(jnp.int32, (nb, rows), 0)
    past = bidx < i
    bidx = bidx.astype(F32)
    gate = jnp.where(past, gate, NEG)
    bias = jnp.full((nb, rows), NEG, F32)
    for _ in range(min(MOBA_TOPK, nb)):
        top = jnp.max(gate, axis=0, keepdims=True)
        first = jnp.min(jnp.where(gate == top, bidx, float(nb)), axis=0, keepdims=True)
        pick = bidx == first
        bias = jnp.where(pick & past, 0.0, bias)
        gate = jnp.where(pick, TAKEN, gate)
    qa_ref[dh:dh + nb, :] = bias.astype(BF16)
    if nb < LANES:
        qa_ref[dh + nb:dh + LANES, :] = jnp.zeros((LANES - nb, rows), BF16)

    s = s_own
    kpos = lax.broadcasted_iota(jnp.int32, (blk, rows), 0)
    qpos = lax.broadcasted_iota(jnp.int32, (blk, rows), 1) & (blk - 1)
    s = jnp.where(kpos <= qpos, s, NEG)
    m = jnp.max(s, axis=0, keepdims=True)
    m_ref[...] = m
    acc_ref[...] = jnp.dot(vt_ref[i], jnp.exp2(s - m).astype(BF16), preferred_element_type=F32)

    def scores(j, s_ref):
        off = pl.multiple_of(jnp.minimum(j, nb - 1) * blk, blk)
        s_ref[...] = jnp.dot(k_ref[pl.ds(off, blk), :], qa_ref[...], preferred_element_type=F32)

    reread = pl.multiple_of(jnp.minimum(i, 0) * blk, blk)

    def update(s_ref, j):
        v_t = vt_ref[jnp.minimum(j, nb - 1)]
        for c in range(rows // MXU_COLS):
            cols = slice(c * MXU_COLS, (c + 1) * MXU_COLS)
            s = s_ref[pl.ds(reread, blk), cols]
            m_old = m_ref[:, cols]
            m_new = jnp.maximum(m_old, jnp.max(s, axis=0, keepdims=True))
            m_ref[:, cols] = m_new
            p = jnp.exp2(s - m_new).astype(BF16)
            acc_ref[:, cols] = jnp.exp2(m_old - m_new) * acc_ref[:, cols] + jnp.dot(
                v_t, p, preferred_element_type=F32)

    scores(0, sa_ref)

    def pair(j):
        scores(j + 1, sb_ref)
        update(sa_ref, j)
        scores(j + 2, sa_ref)
        update(sb_ref, j + 1)

    def quad_trip(t, carry):
        pair(4 * t)
        pair(4 * t + 2)
        return carry
    def oct_trip(t, carry):
        pair(8 * t)
        pair(8 * t + 2)
        pair(8 * t + 4)
        pair(8 * t + 6)
        return carry
    def hex_trip(t, carry):
        for c in range(8):
            pair(16 * t + 2 * c)
        return carry
    n_hexes = i // 16
    lax.fori_loop(0, n_hexes, hex_trip, 0)
    n_octs = i // 8
    lax.fori_loop(2 * n_hexes, n_octs, oct_trip, 0)
    n_quads = i // 4
    lax.fori_loop(2 * n_octs, n_quads, quad_trip, 0)

    def pair_trip(t, carry):
        pair(4 * n_quads + 2 * t)
        return carry
    lax.fori_loop(0, (i - 4 * n_quads + 1) // 2, pair_trip, 0)

    out_t = acc_ref[0:dh, :] / acc_ref[dh:dh + 1, :]
    for g in range(group):
        o_ref[:, g * dh:(g + 1) * dh] = out_t[:, g * blk:(g + 1) * blk].T.astype(o_ref.dtype)


def _moba_attention(q, k, v):
    h, s, dh = q.shape
    hkv = k.shape[0]
    group = h // hkv
    blk = MOBA_BLOCK
    nb = s // blk
    rows = group * blk
    assert nb <= LANES and dh == LANES
    block_of_key = jnp.arange(s, dtype=jnp.int32)[:, None] // blk
    onehot = (block_of_key == jnp.arange(LANES, dtype=jnp.int32)[None, :]).astype(BF16)
    k_aug = jnp.concatenate([k, jnp.broadcast_to(onehot, (hkv, s, LANES))], axis=-1)
    v_t = v.reshape(hkv, nb, blk, dh).transpose(0, 1, 3, 2)
    ones_rows = jnp.zeros((hkv, nb, BF16_SUBLANES, blk), BF16).at[:, :, 0, :].set(1.0)
    vt_aug = jnp.concatenate([v_t, ones_rows], axis=2)
    vrows = dh + BF16_SUBLANES
    return pl.pallas_call(
        _moba_kernel,
        grid=(hkv, nb),
        in_specs=[
            pl.BlockSpec((group, blk, dh), lambda hh, i: (hh, i, 0)),
            pl.BlockSpec((None, s, 2 * LANES), lambda hh, i: (hh, 0, 0)),
            pl.BlockSpec((None, nb, vrows, blk), lambda hh, i: (hh, 0, 0, 0)),
        ],
        out_specs=pl.BlockSpec((blk, group * dh), lambda hh, i: (i, hh)),
        out_shape=jax.ShapeDtypeStruct((s, h * dh), BF16),
        scratch_shapes=[
            pltpu.VMEM((nb, dh), F32),
            pltpu.VMEM((2 * LANES, rows), BF16),
            pltpu.VMEM((blk, rows), F32),
            pltpu.VMEM((blk, rows), F32),
            pltpu.VMEM((1, rows), F32),
            pltpu.VMEM((vrows, rows), F32),
        ],
        compiler_params=_params(
            ("arbitrary", "arbitrary"),
            rows * dh * 2 + s * 2 * LANES * 2 + nb * vrows * blk * 2 + blk * group * dh * 2,
            (2 * blk + vrows + LANES + 1) * rows * 4 + 6 * blk * rows * 4,
            fuse_inputs=[False, True, True]),
        name="moba_attention",
    )(q, k_aug, vt_aug)


def _swa_kernel(q_ref, kp_ref, ko_ref, vp_ref, vo_ref, sink_ref, bias_ref, o_ref, *, nq):
    pairs, span, _ = q_ref.shape
    blk = span // nq
    rows = pairs * blk
    odd = (pl.program_id(0) % 2) == 1
    first_step = pl.program_id(1) == 0

    lane = lax.broadcasted_iota(jnp.int32, (span + blk, LANES), 1)
    lo = lane < B_HEAD_DIM

    def strip(prev_ref, own_ref):
        x = jnp.concatenate([prev_ref[...], own_ref[...]], axis=0)
        swapped = pltpu.roll(x, B_HEAD_DIM, axis=1)
        mine_lo = jnp.where(odd, swapped, x)
        mine_hi = jnp.where(odd, x, swapped)
        return mine_lo, mine_hi

    k_lo, k_hi = strip(kp_ref, ko_ref)
    v_lo, v_hi = strip(vp_ref, vo_ref)
    zero = jnp.zeros_like(k_lo)
    k_both = jnp.where(lo, k_lo, k_hi)
    v_first = jnp.where(lo, v_lo, zero)
    v_second = jnp.where(lo, zero, v_hi)

    qlane = lax.broadcasted_iota(jnp.int32, (rows, LANES), 1)
    q_is_first = qlane < B_HEAD_DIM

    chains = []
    for u in range(nq):
        band = slice(u * blk, (u + 2) * blk)
        q = q_ref[:, u * blk:(u + 1) * blk, :].reshape(rows, LANES)
        qzero = jnp.zeros_like(q)
        bias = bias_ref[1] if u else bias_ref[jnp.where(first_step, 0, 1)]
        for q_half, v_half, sink in (
                (jnp.where(q_is_first, q, qzero), v_first, sink_ref[0:1, :]),
                (jnp.where(q_is_first, qzero, q), v_second, sink_ref[1:2, :])):
            s = lax.dot_general(k_both[band], q_half, _NT, preferred_element_type=F32)
            chains.append((s + bias, v_half[band], sink))
    probs = []
    for s, v_band, sink in chains:
        m = jnp.maximum(jnp.max(s, axis=0, keepdims=True), sink)
        p = jnp.exp2(s - m)
        denom = jnp.sum(p, axis=0, keepdims=True) + jnp.exp2(sink - m)
        probs.append((p.astype(BF16), v_band, denom))
    outs = [lax.dot_general(v_band, p, _TN, preferred_element_type=F32) / denom
            for p, v_band, denom in probs]
    for u in range(nq):
        out_t = outs[2 * u] + outs[2 * u + 1]
        for pr in range(pairs):
            o_ref[u * blk:(u + 1) * blk, pr * LANES:(pr + 1) * LANES] = (
                out_t[:, pr * blk:(pr + 1) * blk].T.astype(o_ref.dtype))


def _swa_attention(q, k, v, sinks):
    npair, s, _ = q.shape
    hkv = k.shape[1] // B_HEAD_DIM
    pairs = npair // hkv
    blk = SWA_BLOCK
    nq = SWA_STEP_BLOCKS
    span = nq * blk
    rows = pairs * blk
    sink_rows = jnp.repeat(
        (sinks.astype(F32) * LOG2_E).reshape(hkv, pairs, 2).transpose(0, 2, 1), blk, axis=-1)
    kk = np.arange(2 * blk)[:, None]
    tt = np.arange(rows)[None, :] % blk
    in_window = (kk > tt) & (kk <= tt + blk)
    band_bias = jnp.asarray(
        np.where(np.stack([in_window & (kk >= blk), in_window]), 0.0, NEG), dtype=F32)
    prev = pl.BlockSpec((blk, LANES), lambda hh, n: (jnp.maximum(n * nq - 1, 0), hh // 2))
    own = pl.BlockSpec((span, LANES), lambda hh, n: (n, hh // 2))
    return pl.pallas_call(
        functools.partial(_swa_kernel, nq=nq),
        grid=(hkv, s // span),
        in_specs=[
            pl.BlockSpec((pairs, span, LANES), lambda hh, n: (hh, n, 0)),
            prev, own, prev, own,
            pl.BlockSpec((None, 2, rows), lambda hh, n: (hh, 0, 0)),
            pl.BlockSpec((2, 2 * blk, rows), lambda hh, n: (0, 0, 0)),
        ],
        out_specs=pl.BlockSpec((span, pairs * LANES), lambda hh, n: (n, hh)),
        out_shape=jax.ShapeDtypeStruct((s, npair * LANES), BF16),
        compiler_params=_params(
            ("parallel", "parallel"),
            2 * pairs * span * LANES * 2 + 2 * (span + blk) * LANES * 2 + 2 * rows * 4
            + 4 * blk * rows * 4,
            12 * blk * rows * 4),
        name="swa_attention",
    )(q, k, k, v, v, sink_rows, band_bias)


def kernel(x, ffn_norm, ffn_w_in, ffn_w_out, attn_norm, a_w_qkv, a_w_o, kv_norm,
           w_kv_shared, b_w_q, b_sinks, b_w_o, final_norm):
    b, s, d = x.shape
    depth = ffn_norm.shape[0]
    n_a = a_w_qkv.shape[0]
    a_q = a_w_o.shape[1]
    a_kv = (a_w_qkv.shape[2] - a_q) // 2
    b_kv = w_kv_shared.shape[1] // 2
    rope_a = _rope_tables(s, A_HEAD_DIM)
    rope_b = _rope_tables(s, B_HEAD_DIM)
    w_in, w_out = ffn_w_in, ffn_w_out
    w_qkv, w_ao = a_w_qkv.astype(BF16), a_w_o.astype(BF16)
    w_kv, w_bq, w_bo = w_kv_shared.astype(BF16), b_w_q.astype(BF16), b_w_o.astype(BF16)

    def gains_entering(layer):
        if layer == depth:
            return []
        return ([kv_norm] if layer == n_a else []) + [ffn_norm[layer, 0]]

    outs = []
    for bi in range(b):
        xs = x.reshape(s, d) if b == 1 else x[bi]
        sumsq, hs = _prenorm(xs, jnp.stack(gains_entering(0)))
        k_sh = v_sh = None
        for layer in range(depth):
            hs = list(hs)
            if layer == n_a:
                h_kv = hs.pop(0)
                k_sh = _proj(h_kv, sumsq, w_kv, 0, b_kv, rope=rope_b, half=B_HEAD_DIM // 2)
                v_sh = _proj(h_kv, sumsq, w_kv, b_kv, b_kv)
            act, w_down = _swiglu_up(hs[0], sumsq, w_in, w_out, (layer, 0))
            xs, sumsq, (h,) = _matmul_residual(act, w_down, (), xs, 0.5, 512, 512,
                                               [attn_norm[layer]])
            if layer < n_a:
                w = w_qkv[layer]
                q = _proj(h, sumsq, w, 0, a_q, rope=rope_a, half=A_HEAD_DIM // 2,
                          scale=A_HEAD_DIM ** -0.5 * LOG2_E, group_major=True)
                k = _proj(h, sumsq, w, a_q, a_kv, rope=rope_a, half=A_HEAD_DIM // 2,
                          group_major=True)
                v = _proj(h, sumsq, w, a_q + a_kv, a_kv, group_major=True)
                o = _moba_attention(q, k, v)
                w_o, lead = w_ao, (layer,)
            else:
                li = layer - n_a
                q = _proj(h, sumsq, w_bq[li], 0, w_bq.shape[2], rope=rope_b,
                          half=B_HEAD_DIM // 2, scale=B_HEAD_DIM ** -0.5 * LOG2_E, group_major=True)
                o = _swa_attention(q, k_sh, v_sh, b_sinks[li])
                w_o, lead = w_bo, (li,)
            xs, sumsq, (h,) = _matmul_residual(o, w_o, lead, xs, 1.0, 1024, 512,
                                               [ffn_norm[layer, 1]])
            act, w_down = _swiglu_up(h, sumsq, w_in, w_out, (layer, 1))
            xs, sumsq, hs = _matmul_residual(act, w_down, (), xs, 0.5, 512, 512,
                                             gains_entering(layer + 1))
        outs.append(_final_norm(xs, sumsq, final_norm))
    return outs[0].reshape(b, s, d) if b == 1 else jnp.stack(outs, axis=0)
```
